```python
import jax
import jax.numpy as jnp
from jax import lax
import numpy as np

D_MODEL = 1024
BATCH = 16
SEQ = 2048
DEPTH = 2

HEAD_DIM = 64
LRU_WIDTH = D_MODEL // 2
LRU_BLOCKS = 8
LRU_BLOCK_DIM = LRU_WIDTH // LRU_BLOCKS
CONV_WIDTH = 4
LRU_C = 8.0
NSA_HEADS = D_MODEL // 128
NSA_KV_HEADS = 2
NSA_GROUP = NSA_HEADS // NSA_KV_HEADS
NSA_WIDTH = NSA_HEADS * HEAD_DIM
NSA_KV_WIDTH = NSA_KV_HEADS * HEAD_DIM
CMP_LEN = 32
CMP_STRIDE = 16
SLC_LEN = 64
N_SELECT = 16
WINDOW = 512
SLC_Q_CHUNK = 32
FOX_HEADS = D_MODEL // 128
FOX_WIDTH = FOX_HEADS * HEAD_DIM
Q_BLOCK = 128
D_MIX = LRU_WIDTH + NSA_WIDTH + FOX_WIDTH
IN_SPLITS = (LRU_WIDTH, LRU_WIDTH,
             NSA_WIDTH, NSA_KV_WIDTH, NSA_KV_WIDTH, NSA_KV_WIDTH, NSA_KV_WIDTH, NSA_KV_WIDTH, NSA_KV_WIDTH,
             3 * NSA_HEADS, NSA_WIDTH,
             FOX_WIDTH, FOX_WIDTH, FOX_WIDTH, FOX_HEADS, FOX_WIDTH)
D_IN = sum(IN_SPLITS)
IN_OFFSETS = tuple(int(o) for o in np.cumsum(IN_SPLITS)[:-1])
NORM_EPS = 1e-6
NEG_INF = -1e30
FORCE_SCORE = 1e9
ATTN_SCALE = HEAD_DIM ** -0.5

kernel_name = 'hybrid_rglru_nsa_fox_block'


def rmsnorm(x, g):
    x32 = x.astype(jnp.float32)
    y = x32 * lax.rsqrt(jnp.mean(x32 * x32, axis=-1, keepdims=True) + NORM_EPS)
    return (y * g.astype(jnp.float32)).astype(x.dtype)


def split_heads(t, n_heads):
    return t.reshape(t.shape[0], t.shape[1], n_heads, HEAD_DIM)


def safe_softmax(s, mask):
    s = jnp.where(mask, s.astype(jnp.float32), NEG_INF)
    m = jnp.max(s, axis=-1, keepdims=True)
    e = jnp.where(mask, jnp.exp(s - m), 0.0)
    return e / jnp.maximum(jnp.sum(e, axis=-1, keepdims=True), 1e-30)


def linear_combine(c1, c2):
    a1, b1 = c1
    a2, b2 = c2
    return a1 * a2, a2 * b1 + b2


def rglru_group(u, z, conv_w, conv_b, wa, ba, wx, bx, lam):
    bsz, seq, width = u.shape
    u_pad = jnp.pad(u, ((0, 0), (CONV_WIDTH - 1, 0), (0, 0)))
    xc = conv_b + sum(u_pad[:, k:k + seq] * conv_w[k] for k in range(CONV_WIDTH))
    xh = xc.reshape(bsz, seq, LRU_BLOCKS, LRU_BLOCK_DIM)
    r = jax.nn.sigmoid(jnp.einsum('bshi,hij->bshj', xh, wa).reshape(bsz, seq, width) + ba)
    i = jax.nn.sigmoid(jnp.einsum('bshi,hij->bshj', xh, wx).reshape(bsz, seq, width) + bx)
    log_a = -LRU_C * r.astype(jnp.float32) * jax.nn.softplus(-lam.astype(jnp.float32))
    a = jnp.exp(log_a)
    b = jnp.sqrt(-jnp.expm1(2.0 * log_a)) * (i * xc).astype(jnp.float32)
    _, h = lax.associative_scan(linear_combine, (a, b), axis=1)
    return h.astype(u.dtype) * jax.nn.silu(z)


def nsa_compressed(q, kc, vc, pe_k, pe_v, wck, wcv, k_g):
    bsz, seq = kc.shape[0], kc.shape[1]
    n_cmp = (seq - CMP_LEN) // CMP_STRIDE + 1
    n_slc = seq // SLC_LEN
    starts = jnp.arange(n_cmp) * CMP_STRIDE
    tok = starts[:, None] + jnp.arange(CMP_LEN)[None, :]
    k_cmp = jnp.einsum('bnlgd,lde->bnge', kc[:, tok] + pe_k[:, None, :], wck)
    v_cmp = jnp.einsum('bnlgd,lde->bnge', vc[:, tok] + pe_v[:, None, :], wcv)
    k_cmp = rmsnorm(k_cmp, k_g)
    qg = q.reshape(bsz, seq, NSA_KV_HEADS, NSA_GROUP, HEAD_DIM)
    s = jnp.einsum('bsgrd,bngd->bgrsn', qg, k_cmp) * ATTN_SCALE
    t = jnp.arange(seq)
    mask = (starts + CMP_LEN - 1)[None, :] <= t[:, None]
    p = safe_softmax(s, mask)
    o = jnp.einsum('bgrsn,bngd->bsgrd', p, v_cmp).reshape(bsz, seq, NSA_HEADS, HEAD_DIM)
    slc_starts = jnp.arange(n_slc) * SLC_LEN
    overlap = ((starts[:, None] <= slc_starts[None, :] + SLC_LEN - 1)
               & (starts[:, None] + CMP_LEN - 1 >= slc_starts[None, :])).astype(jnp.float32)
    importance = jnp.einsum('bgrsn,nj->bgsj', p, overlap)
    return o.astype(q.dtype), importance


def nsa_selected(q, ks, vs, importance):
    bsz, seq = ks.shape[0], ks.shape[1]
    n_slc = seq // SLC_LEN
    k_sel = min(N_SELECT, n_slc)
    t = jnp.arange(seq)
    blk_t = t // SLC_LEN
    j = jnp.arange(n_slc)
    valid = j[None, :] <= blk_t[:, None]
    forced = (j[None, :] == 0) | (j[None, :] == blk_t[:, None]) | (j[None, :] == blk_t[:, None] - 1)
    score = jnp.where(valid, jnp.where(forced, FORCE_SCORE, importance), NEG_INF)
    vals, idx = lax.top_k(score, k_sel)
    sel_ok = vals > 0.5 * NEG_INF
    ks_blk = ks.reshape(bsz, n_slc, SLC_LEN, NSA_KV_HEADS, HEAD_DIM).transpose(0, 3, 1, 2, 4)
    vs_blk = vs.reshape(bsz, n_slc, SLC_LEN, NSA_KV_HEADS, HEAD_DIM).transpose(0, 3, 1, 2, 4)
    n_chunk = seq // SLC_Q_CHUNK
    q_c = q.reshape(bsz, n_chunk, SLC_Q_CHUNK, NSA_KV_HEADS, NSA_GROUP, HEAD_DIM).transpose(1, 0, 2, 3, 4, 5)
    idx_c = idx.reshape(bsz, NSA_KV_HEADS, n_chunk, SLC_Q_CHUNK, k_sel).transpose(2, 0, 1, 3, 4)
    ok_c = sel_ok.reshape(bsz, NSA_KV_HEADS, n_chunk, SLC_Q_CHUNK, k_sel).transpose(2, 0, 1, 3, 4)
    t_c = t.reshape(n_chunk, SLC_Q_CHUNK)
    bi = jnp.arange(bsz)[:, None, None, None]
    gi = jnp.arange(NSA_KV_HEADS)[None, :, None, None]

    def chunk(args):
        qb, ib, okb, tb = args
        kk = ks_blk[bi, gi, ib]
        vv = vs_blk[bi, gi, ib]
        s = jnp.einsum('bcgrd,bgcknd->bgrckn', qb, kk) * ATTN_SCALE
        pos = ib[..., None] * SLC_LEN + jnp.arange(SLC_LEN)
        mask = okb[..., None] & (pos <= tb[None, None, :, None, None])
        s = s.reshape(bsz, NSA_KV_HEADS, NSA_GROUP, SLC_Q_CHUNK, k_sel * SLC_LEN)
        mask = mask.reshape(bsz, NSA_KV_HEADS, 1, SLC_Q_CHUNK, k_sel * SLC_LEN)
        p = safe_softmax(s, mask)
        vv = vv.reshape(bsz, NSA_KV_HEADS, SLC_Q_CHUNK, k_sel * SLC_LEN, HEAD_DIM)
        return jnp.einsum('bgrcm,bgcmd->bcgrd', p, vv).astype(q.dtype)

    o = lax.map(chunk, (q_c, idx_c, ok_c, t_c))
    return o.transpose(1, 0, 2, 3, 4, 5).reshape(bsz, seq, NSA_HEADS, HEAD_DIM)


def nsa_window(q, kw, vw):
    bsz, seq = kw.shape[0], kw.shape[1]
    n_blk = seq // Q_BLOCK
    span = Q_BLOCK + WINDOW
    k_pad = jnp.pad(kw, ((0, 0), (WINDOW, 0), (0, 0), (0, 0)))
    v_pad = jnp.pad(vw, ((0, 0), (WINDOW, 0), (0, 0), (0, 0)))
    q_b = q.reshape(bsz, n_blk, Q_BLOCK, NSA_KV_HEADS, NSA_GROUP, HEAD_DIM).transpose(1, 0, 2, 3, 4, 5)

    def blk(args):
        b_idx, qb = args
        start = b_idx * Q_BLOCK
        kb = lax.dynamic_slice_in_dim(k_pad, start, span, axis=1)
        vb = lax.dynamic_slice_in_dim(v_pad, start, span, axis=1)
        t = start + jnp.arange(Q_BLOCK)
        s_pos = start - WINDOW + jnp.arange(span)
        diff = t[:, None] - s_pos[None, :]
        mask = (s_pos[None, :] >= 0) & (diff >= 0) & (diff < WINDOW)
        s = jnp.einsum('bqgrd,bkgd->bgrqk', qb, kb) * ATTN_SCALE
        p = safe_softmax(s, mask)
        return jnp.einsum('bgrqk,bkgd->bqgrd', p, vb).astype(q.dtype)

    o = lax.map(blk, (jnp.arange(n_blk), q_b))
    return o.transpose(1, 0, 2, 3, 4, 5).reshape(bsz, seq, NSA_HEADS, HEAD_DIM)


def nsa_group(q, kc, vc, ks, vs, kw, vw, gate_logits, z, q_g, k_g, pe_k, pe_v, wck, wcv, gate_b):
    bsz, seq = q.shape[0], q.shape[1]
    q = rmsnorm(split_heads(q, NSA_HEADS), q_g)
    kc, vc, ks, vs, kw, vw = (split_heads(t, NSA_KV_HEADS) for t in (kc, vc, ks, vs, kw, vw))
    o_cmp, importance = nsa_compressed(q, kc, vc, pe_k, pe_v, wck, wcv, k_g[0])
    o_slc = nsa_selected(q, rmsnorm(ks, k_g[1]), vs, importance)
    o_win = nsa_window(q, rmsnorm(kw, k_g[2]), vw)
    g = jax.nn.sigmoid(gate_logits + gate_b).reshape(bsz, seq, NSA_HEADS, 3)
    o = g[..., 0:1] * o_cmp + g[..., 1:2] * o_slc + g[..., 2:3] * o_win
    return o.reshape(bsz, seq, NSA_WIDTH).astype(z.dtype) * jax.nn.silu(z)


def fox_group(q, k, v, f_logits, z, q_g, k_g, f_b):
    bsz, seq = q.shape[0], q.shape[1]
    q = rmsnorm(split_heads(q, FOX_HEADS), q_g)
    k = rmsnorm(split_heads(k, FOX_HEADS), k_g)
    v = split_heads(v, FOX_HEADS)
    log_f = jax.nn.log_sigmoid(f_logits.astype(jnp.float32) + f_b.astype(jnp.float32))
    c = jnp.cumsum(log_f, axis=1).transpose(0, 2, 1)
    n_blk = seq // Q_BLOCK
    q_b = q.reshape(bsz, n_blk, Q_BLOCK, FOX_HEADS, HEAD_DIM).transpose(1, 0, 2, 3, 4)
    c_b = c.reshape(bsz, FOX_HEADS, n_blk, Q_BLOCK).transpose(2, 0, 1, 3)
    s_pos = jnp.arange(seq)

    def blk(args):
        b_idx, qb, cb = args
        t = b_idx * Q_BLOCK + jnp.arange(Q_BLOCK)
        s = (jnp.einsum('bqhd,bshd->bhqs', qb, k).astype(jnp.float32) * ATTN_SCALE
             + cb[..., None] - c[:, :, None, :])
        mask = s_pos[None, :] <= t[:, None]
        p = safe_softmax(s, mask)
        return jnp.einsum('bhqs,bshd->bqhd', p, v).astype(q.dtype)

    o = lax.map(blk, (jnp.arange(n_blk), q_b, c_b))
    o = o.transpose(1, 0, 2, 3, 4).reshape(bsz, seq, FOX_WIDTH)
    return o.astype(z.dtype) * jax.nn.silu(z)


def hybrid_layer(x, norm_g, w_in, w_out, conv_w, conv_b, lru_wa, lru_ba, lru_wx, lru_bx, lru_lambda,
                 nsa_q_g, nsa_k_g, cmp_pe_k, cmp_pe_v, cmp_wk, cmp_wv, nsa_gate_b,
                 fox_q_g, fox_k_g, fox_f_b):
    h = rmsnorm(x, norm_g)
    proj = jnp.einsum('bsd,de->bse', h, w_in)
    (lru_u, lru_z, nsa_q, nsa_kc, nsa_vc, nsa_ks, nsa_vs, nsa_kw, nsa_vw, nsa_gl, nsa_z,
     fox_q, fox_k, fox_v, fox_fl, fox_z) = jnp.split(proj, IN_OFFSETS, axis=-1)
    y_lru = rglru_group(lru_u, lru_z, conv_w, conv_b, lru_wa, lru_ba, lru_wx, lru_bx, lru_lambda)
    y_nsa = nsa_group(nsa_q, nsa_kc, nsa_vc, nsa_ks, nsa_vs, nsa_kw, nsa_vw, nsa_gl, nsa_z,
                      nsa_q_g, nsa_k_g, cmp_pe_k, cmp_pe_v, cmp_wk, cmp_wv, nsa_gate_b)
    y_fox = fox_group(fox_q, fox_k, fox_v, fox_fl, fox_z, fox_q_g, fox_k_g, fox_f_b)
    y = jnp.concatenate([y_lru, y_nsa, y_fox], axis=-1)
    return x + jnp.einsum('bse,ed->bsd', y, w_out)


def setup_inputs(seed: int = 0) -> dict:
    key = jax.random.key(seed)
    ks = jax.random.split(key, 22)
    f32 = jnp.float32

    def nrm(k, shape, scale):
        return jax.random.normal(k, shape, f32) * scale

    L = DEPTH
    a0 = jax.random.uniform(ks[10], (L, LRU_WIDTH), f32, 0.9, 0.999)
    return {
        'x': nrm(ks[0], (BATCH, SEQ, D_MODEL), 1.0),
        'norm_g': 1.0 + nrm(ks[1], (L, D_MODEL), 0.1),
        'w_in': nrm(ks[2], (L, D_MODEL, D_IN), D_MODEL ** -0.5),
        'w_out': nrm(ks[3], (L, D_MIX, D_MODEL), D_MIX ** -0.5),
        'conv_w': nrm(ks[4], (L, CONV_WIDTH, LRU_WIDTH), CONV_WIDTH ** -0.5),
        'conv_b': nrm(ks[5], (L, LRU_WIDTH), 0.02),
        'lru_wa': nrm(ks[6], (L, LRU_BLOCKS, LRU_BLOCK_DIM, LRU_BLOCK_DIM), LRU_BLOCK_DIM ** -0.5),
        'lru_ba': nrm(ks[7], (L, LRU_WIDTH), 0.1),
        'lru_wx': nrm(ks[8], (L, LRU_BLOCKS, LRU_BLOCK_DIM, LRU_BLOCK_DIM), LRU_BLOCK_DIM ** -0.5),
        'lru_bx': nrm(ks[9], (L, LRU_WIDTH), 0.1),
        'lru_lambda': jnp.log(a0) - jnp.log1p(-a0),
        'nsa_q_g': 1.0 + nrm(ks[11], (L, HEAD_DIM), 0.1),
        'nsa_k_g': 1.0 + nrm(ks[12], (L, 3, HEAD_DIM), 0.1),
        'cmp_pe_k': nrm(ks[13], (L, CMP_LEN, HEAD_DIM), 0.1),
        'cmp_pe_v': nrm(ks[14], (L, CMP_LEN, HEAD_DIM), 0.1),
        'cmp_wk': nrm(ks[15], (L, CMP_LEN, HEAD_DIM, HEAD_DIM), (CMP_LEN * HEAD_DIM) ** -0.5),
        'cmp_wv': nrm(ks[16], (L, CMP_LEN, HEAD_DIM, HEAD_DIM), (CMP_LEN * HEAD_DIM) ** -0.5),
        'nsa_gate_b': nrm(ks[17], (L, 3 * NSA_HEADS), 0.1),
        'fox_q_g': 1.0 + nrm(ks[18], (L, HEAD_DIM), 0.1),
        'fox_k_g': 1.0 + nrm(ks[19], (L, HEAD_DIM), 0.1),
        'fox_f_b': 2.0 + nrm(ks[20], (L, FOX_HEADS), 0.5),
    }


def reference(x, norm_g, w_in, w_out, conv_w, conv_b, lru_wa, lru_ba, lru_wx, lru_bx, lru_lambda,
              nsa_q_g, nsa_k_g, cmp_pe_k, cmp_pe_v, cmp_wk, cmp_wv, nsa_gate_b,
              fox_q_g, fox_k_g, fox_f_b):
    for l in range(DEPTH):
        x = hybrid_layer(x, norm_g[l], w_in[l], w_out[l], conv_w[l], conv_b[l], lru_wa[l], lru_ba[l],
                         lru_wx[l], lru_bx[l], lru_lambda[l], nsa_q_g[l], nsa_k_g[l], cmp_pe_k[l],
                         cmp_pe_v[l], cmp_wk[l], cmp_wv[l], nsa_gate_b[l], fox_q_g[l], fox_k_g[l],
                         fox_f_b[l])
    return x
```

```python
import functools

import jax
import jax.numpy as jnp
import numpy as np
from jax import lax
from jax.experimental import pallas as pl
from jax.experimental.pallas import tpu as pltpu

HEAD_DIM = 64
LRU_BLOCKS = 8
CONV_WIDTH = 4
LRU_C = 8.0
NSA_KV_HEADS = 2
NSA_GROUP = 4
CMP_LEN = 32
CMP_STRIDE = 16
SLC_LEN = 64
N_SELECT = 16
WINDOW = 512
NORM_EPS = 1e-6
NEG_INF = -1e30
FORCE_SCORE = 1e9
ATTN_SCALE = HEAD_DIM ** -0.5

LANES = 128
SUBLANES = 8
TQ = 256
TK = 128
PROJ_ROWS = 256
LRU_ROWS = 512
VMEM_LIMIT = 56 * 1024 * 1024

GATE_GROUP_STRIDE = 16
FORGET_LANE0 = 32

_NT = (((1,), (1,)), ((), ()))


def _dot(a, b):
    return jnp.dot(a, b, preferred_element_type=jnp.float32)


def _dot_nt(a, b):
    return lax.dot_general(a, b, _NT, preferred_element_type=jnp.float32)


def _split3(x):
    hi = x.astype(jnp.bfloat16)
    r = x - hi.astype(jnp.float32)
    mid = r.astype(jnp.bfloat16)
    lo = (r - mid.astype(jnp.float32)).astype(jnp.bfloat16)
    return hi, mid, lo


def _dot01_left(m01, x):
    hi, mid, lo = _split3(x)
    return _dot(m01, hi) + _dot(m01, mid) + _dot(m01, lo)


def _dot01_right(x, m01):
    hi, mid, lo = _split3(x)
    return _dot(hi, m01) + _dot(mid, m01) + _dot(lo, m01)


def _head_rmsnorm(x, gain):
    ms = jnp.sum(x * x, axis=-1, keepdims=True) * (1.0 / HEAD_DIM)
    return x * lax.rsqrt(ms + NORM_EPS) * gain


def _sigmoid(x):
    return 1.0 / (1.0 + jnp.exp(-x))


def _silu(x):
    return x * _sigmoid(x)


def _params(sem):
    return pltpu.CompilerParams(dimension_semantics=sem, vmem_limit_bytes=VMEM_LIMIT)


def _in_proj_kernel(seg_bounds, x_ref, g_ref, w_ref, *out_refs):
    x = x_ref[...]
    ms = jnp.mean(x * x, axis=-1, keepdims=True)
    h = (x * lax.rsqrt(ms + NORM_EPS) * g_ref[...]).astype(jnp.bfloat16)
    for (lo, hi), o_ref in zip(seg_bounds, out_refs):
        o_ref[...] = _dot(h, w_ref[:, lo:hi]).astype(o_ref.dtype)


def _in_proj(x2d, norm_g, w_perm, seg_widths):
    n, d = x2d.shape
    bounds, off = [], 0
    for w in seg_widths:
        bounds.append((off, off + w))
        off += w
    assert off == w_perm.shape[1] and n % PROJ_ROWS == 0
    return pl.pallas_call(
        functools.partial(_in_proj_kernel, tuple(bounds)),
        out_shape=[jax.ShapeDtypeStruct((n, w), jnp.float32) for w in seg_widths],
        grid=(n // PROJ_ROWS,),
        in_specs=[pl.BlockSpec((PROJ_ROWS, d), lambda i: (i, 0)),
                  pl.BlockSpec((1, d), lambda i: (0, 0)),
                  pl.BlockSpec(w_perm.shape, lambda i: (0, 0))],
        out_specs=[pl.BlockSpec((PROJ_ROWS, w), lambda i: (i, 0)) for w in seg_widths],
        compiler_params=_params(("parallel",)),
        name="in_proj",
    )(x2d, norm_g.reshape(1, d), w_perm)


def _out_proj_kernel(x_ref, ya_ref, yb_ref, yc_ref, w_ref, o_ref):
    wa = ya_ref.shape[1]
    wb = yb_ref.shape[1]
    acc = x_ref[...]
    acc = acc + _dot(ya_ref[...], w_ref[0:wa, :])
    acc = acc + _dot(yb_ref[...], w_ref[wa:wa + wb, :])
    acc = acc + _dot(yc_ref[...], w_ref[wa + wb:, :])
    o_ref[...] = acc


def _out_proj(x2d, ya, yb, yc, w_out_bf16):
    n, d = x2d.shape
    row = lambda i: (i, 0)
    return pl.pallas_call(
        _out_proj_kernel,
        out_shape=jax.ShapeDtypeStruct((n, d), jnp.float32),
        grid=(n // PROJ_ROWS,),
        in_specs=[pl.BlockSpec((PROJ_ROWS, d), row),
                  pl.BlockSpec((PROJ_ROWS, ya.shape[1]), row),
                  pl.BlockSpec((PROJ_ROWS, yb.shape[1]), row),
                  pl.BlockSpec((PROJ_ROWS, yc.shape[1]), row),
                  pl.BlockSpec(w_out_bf16.shape, lambda i: (0, 0))],
        out_specs=pl.BlockSpec((PROJ_ROWS, d), row),
        compiler_params=_params(("parallel",)),
        name="out_proj",
    )(x2d, ya, yb, yc, w_out_bf16)


def _lru_kernel(uz_ref, cw_ref, cb_ref, wg_ref, bg_ref, lam_ref, y_ref, ubuf, a_s, b_s, hc):
    rows = a_s.shape[0]
    width = a_s.shape[1]

    @pl.when(pl.program_id(1) == 0)
    def _():
        ubuf[0:SUBLANES, :] = jnp.zeros((SUBLANES, width), jnp.float32)
        hc[...] = jnp.zeros_like(hc)

    u = uz_ref[0, :, 0:width]
    z = uz_ref[0, :, width:2 * width]
    ubuf[SUBLANES:SUBLANES + rows, :] = u
    xc = cb_ref[...] + cw_ref[CONV_WIDTH - 1:CONV_WIDTH, :] * u
    for k in range(CONV_WIDTH - 1):
        shift = CONV_WIDTH - 1 - k
        xc = xc + cw_ref[k:k + 1, :] * ubuf[SUBLANES - shift:SUBLANES - shift + rows, :]
    ubuf[0:SUBLANES, :] = u[rows - SUBLANES:rows, :]

    gates = _dot(xc.astype(jnp.bfloat16), wg_ref[...]) + bg_ref[...]
    r = _sigmoid(gates[:, 0:width])
    ig = _sigmoid(gates[:, width:2 * width])
    nlam = -lam_ref[...]
    softplus = jnp.maximum(nlam, 0.0) + jnp.log1p(jnp.exp(-jnp.abs(nlam)))
    log_a = (-LRU_C) * r * softplus
    a = jnp.exp(log_a)
    a_s[...] = a
    b_s[...] = jnp.sqrt(1.0 - a * a) * (ig * xc)

    row = lax.broadcasted_iota(jnp.int32, (SUBLANES, width), 0)

    def group(gi, h_prev):
        r0 = pl.multiple_of(gi * SUBLANES, SUBLANES)
        a8 = a_s[pl.ds(r0, SUBLANES), :]
        b8 = b_s[pl.ds(r0, SUBLANES), :]
        d = 1
        while d < SUBLANES:
            a_sh = jnp.where(row >= d, pltpu.roll(a8, d, 0), 1.0)
            b_sh = jnp.where(row >= d, pltpu.roll(b8, d, 0), 0.0)
            b8 = b8 + a8 * b_sh
            a8 = a8 * a_sh
            d *= 2
        h8 = b8 + a8 * h_prev
        b_s[pl.ds(r0, SUBLANES), :] = h8
        return h8[SUBLANES - 1:SUBLANES, :]

    hc[...] = lax.fori_loop(0, rows // SUBLANES, group, hc[...])
    y_ref[0] = (b_s[...] * _silu(z)).astype(y_ref.dtype)


def _lru(uz, conv_w, conv_b, w_gates, b_gates, lam):
    bsz, seq, w2 = uz.shape
    width = w2 // 2
    rows = min(LRU_ROWS, seq)
    assert seq % rows == 0
    const = lambda b, i: (0, 0)
    return pl.pallas_call(
        _lru_kernel,
        out_shape=jax.ShapeDtypeStruct((bsz, seq, width), jnp.bfloat16),
        grid=(bsz, seq // rows),
        in_specs=[pl.BlockSpec((1, rows, w2), lambda b, i: (b, i, 0)),
                  pl.BlockSpec(conv_w.shape, const),
                  pl.BlockSpec((1, width), const),
                  pl.BlockSpec(w_gates.shape, const),
                  pl.BlockSpec((1, w2), const),
                  pl.BlockSpec((1, width), const)],
        out_specs=pl.BlockSpec((1, rows, width), lambda b, i: (b, i, 0)),
        scratch_shapes=[pltpu.VMEM((rows + SUBLANES, width), jnp.float32),
                        pltpu.VMEM((rows, width), jnp.float32),
                        pltpu.VMEM((rows, width), jnp.float32),
                        pltpu.VMEM((1, width), jnp.float32)],
        compiler_params=_params(("parallel", "arbitrary")),
        name="rglru",
    )(uz, conv_w, conv_b.reshape(1, width), w_gates, b_gates.reshape(1, w2), lam.reshape(1, width))


def _cmp_kernel(q_ref, kc_ref, vc_ref, qg_ref, kg_ref, pek_ref, pev_ref, wk_ref, wv_ref,
                qn_ref, ocmp_ref, sel_ref):
    seq = q_ref.shape[1]
    n_heads = q_ref.shape[2] // HEAD_DIM
    n_blk = seq // CMP_STRIDE
    n_cmp = (seq - CMP_LEN) // CMP_STRIDE + 1
    n_slc = seq // SLC_LEN
    half = CMP_LEN // CMP_STRIDE
    assert half == 2 and n_blk == LANES
    kvw = NSA_KV_HEADS * HEAD_DIM

    acc = [jnp.zeros((n_blk, kvw), jnp.float32) for _ in range(4)]
    for l in range(CMP_STRIDE):
        xk = kc_ref[0, pl.ds(l, n_blk, stride=CMP_STRIDE), :]
        xv = vc_ref[0, pl.ds(l, n_blk, stride=CMP_STRIDE), :]
        l2 = CMP_STRIDE + l
        acc[0] = acc[0] + _dot((xk + pek_ref[l:l + 1, :]).astype(jnp.bfloat16), wk_ref[l])
        acc[1] = acc[1] + _dot((xk + pek_ref[l2:l2 + 1, :]).astype(jnp.bfloat16), wk_ref[l2])
        acc[2] = acc[2] + _dot((xv + pev_ref[l:l + 1, :]).astype(jnp.bfloat16), wv_ref[l])
        acc[3] = acc[3] + _dot((xv + pev_ref[l2:l2 + 1, :]).astype(jnp.bfloat16), wv_ref[l2])
    k_cmp = acc[0] + pltpu.roll(acc[1], n_blk - 1, 0)
    v_cmp = acc[2] + pltpu.roll(acc[3], n_blk - 1, 0)
    v_cmp_t = v_cmp.T.astype(jnp.bfloat16)
    k_n = [_head_rmsnorm(k_cmp[:, g * HEAD_DIM:(g + 1) * HEAD_DIM], kg_ref[...]).astype(jnp.bfloat16)
           for g in range(NSA_KV_HEADS)]
    v_t = [v_cmp_t[g * HEAD_DIM:(g + 1) * HEAD_DIM, :] for g in range(NSA_KV_HEADS)]

    n_idx = lax.broadcasted_iota(jnp.int32, (n_blk, TQ), 0)
    j_idx = lax.broadcasted_iota(jnp.int32, (n_slc, TQ), 0)
    oj = lax.broadcasted_iota(jnp.int32, (n_slc, n_blk), 0) * SLC_LEN
    on = lax.broadcasted_iota(jnp.int32, (n_slc, n_blk), 1) * CMP_STRIDE
    overlap_t = ((on <= oj + SLC_LEN - 1) & (on + CMP_LEN - 1 >= oj)
                 & (on < n_cmp * CMP_STRIDE)).astype(jnp.bfloat16)

    def q_tile(qi, carry):
        t0 = pl.multiple_of(qi * TQ, TQ)
        t_row = t0 + lax.broadcasted_iota(jnp.int32, (1, TQ), 1)
        valid_cmp = (n_idx * CMP_STRIDE + CMP_LEN - 1 <= t_row) & (n_idx < n_cmp)
        blk_t = t_row // SLC_LEN
        valid_slc = j_idx <= blk_t
        forced = (j_idx == 0) | (j_idx == blk_t) | (j_idx == blk_t - 1)
        for g in range(NSA_KV_HEADS):
            p_sum = jnp.zeros((n_blk, TQ), jnp.float32)
            for r in range(NSA_GROUP):
                h = g * NSA_GROUP + r
                qh = q_ref[0, pl.ds(t0, TQ), h * HEAD_DIM:(h + 1) * HEAD_DIM]
                qh = (_head_rmsnorm(qh, qg_ref[...]) * ATTN_SCALE).astype(jnp.bfloat16)
                qn_ref[0, h, pl.ds(t0, TQ), :] = qh
                s = _dot_nt(k_n[g], qh)
                s = jnp.where(valid_cmp, s, NEG_INF)
                m = jnp.max(s, axis=0, keepdims=True)
                e = jnp.where(valid_cmp, jnp.exp(s - m), 0.0)
                p = e / jnp.maximum(jnp.sum(e, axis=0, keepdims=True), 1e-30)
                ocmp_ref[0, qi, h * HEAD_DIM:(h + 1) * HEAD_DIM, :] = _dot(v_t[g], p.astype(jnp.bfloat16))
                p_sum = p_sum + p
            imp = _dot01_left(overlap_t, p_sum)
            score = jnp.where(valid_slc, jnp.where(forced, FORCE_SCORE, imp), NEG_INF)
            rank = jnp.zeros((n_slc, TQ), jnp.float32)
            for jp in range(n_slc):
                other = score[jp:jp + 1, :]
                ahead = (other > score) | ((other == score) & (j_idx > jp))
                rank = rank + jnp.where(ahead, 1.0, 0.0)
            sel = jnp.where((rank < N_SELECT) & valid_slc, 1.0, 0.0)
            for j in range(n_slc):
                sel_ref[0, g, qi, j] = sel[j:j + 1, :]
        return carry

    lax.fori_loop(0, seq // TQ, q_tile, 0)


def _nsa_cmp(nq, kc, vc, q_g, k_g0, pe_k2, pe_v2, wk_bd, wv_bd):
    bsz, seq, qw = nq.shape
    n_heads = qw // HEAD_DIM
    n_slc = seq // SLC_LEN
    nq_t = seq // TQ
    c2 = lambda b: (0, 0)
    c3 = lambda b: (0, 0, 0)
    return pl.pallas_call(
        _cmp_kernel,
        out_shape=[jax.ShapeDtypeStruct((bsz, n_heads, seq, HEAD_DIM), jnp.bfloat16),
                   jax.ShapeDtypeStruct((bsz, nq_t, qw, TQ), jnp.float32),
                   jax.ShapeDtypeStruct((bsz, NSA_KV_HEADS, nq_t, n_slc, 1, TQ), jnp.float32)],
        grid=(bsz,),
        in_specs=[pl.BlockSpec((1, seq, qw), lambda b: (b, 0, 0)),
                  pl.BlockSpec((1, seq, kc.shape[2]), lambda b: (b, 0, 0)),
                  pl.BlockSpec((1, seq, vc.shape[2]), lambda b: (b, 0, 0)),
                  pl.BlockSpec((1, HEAD_DIM), c2),
                  pl.BlockSpec((1, HEAD_DIM), c2),
                  pl.BlockSpec(pe_k2.shape, c2),
                  pl.BlockSpec(pe_v2.shape, c2),
                  pl.BlockSpec(wk_bd.shape, c3),
                  pl.BlockSpec(wv_bd.shape, c3)],
        out_specs=[pl.BlockSpec((1, n_heads, seq, HEAD_DIM), lambda b: (b, 0, 0, 0)),
                   pl.BlockSpec((1, nq_t, qw, TQ), lambda b: (b, 0, 0, 0)),
                   pl.BlockSpec((1, NSA_KV_HEADS, nq_t, n_slc, 1, TQ), lambda b: (b, 0, 0, 0, 0, 0))],
        compiler_params=_params(("parallel",)),
        name="nsa_compress",
    )(nq, kc, vc, q_g.reshape(1, HEAD_DIM), k_g0.reshape(1, HEAD_DIM), pe_k2, pe_v2, wk_bd, wv_bd)


def _flash_tile(q, k, v_t, state, bias=None, keep=None):
    m, l, acc = state
    s = _dot_nt(k, q)
    if bias is not None:
        s = s + bias
    if keep is not None:
        s = jnp.where(keep, s, NEG_INF)
    m_new = jnp.maximum(m, jnp.max(s, axis=0, keepdims=True))
    p = jnp.exp(s - m_new)
    if keep is not None:
        p = jnp.where(keep, p, 0.0)
    alpha = jnp.exp(m - m_new)
    l = alpha * l + jnp.sum(p, axis=0, keepdims=True)
    acc = alpha * acc + _dot(v_t, p.astype(jnp.bfloat16))
    return m_new, l, acc


def _flash_init():
    return (jnp.full((1, TQ), NEG_INF, jnp.float32), jnp.zeros((1, TQ), jnp.float32),
            jnp.zeros((HEAD_DIM, TQ), jnp.float32))


def _flash_out(state):
    _, l, acc = state
    return acc / jnp.maximum(l, 1e-30)


def _nsa_kernel(qn_ref, kv_ref, z_ref, ocmp_ref, sel_ref, misc_ref, gb_ref, kg_ref, y_ref,
                ks_s, kw_s, vs_t, vw_t, gate_t):
    seq = kv_ref.shape[1]
    n_kt = seq // TK
    g_id = pl.program_id(1)

    for c in range(n_kt):
        rows = slice(c * TK, (c + 1) * TK)
        ks_s[rows, :] = _head_rmsnorm(kv_ref[0, rows, 0:HEAD_DIM], kg_ref[0:1, :]).astype(jnp.bfloat16)
        kw_s[rows, :] = _head_rmsnorm(kv_ref[0, rows, 2 * HEAD_DIM:3 * HEAD_DIM],
                                      kg_ref[1:2, :]).astype(jnp.bfloat16)
        ksvs_t = kv_ref[0, rows, 0:LANES].T
        kwvw_t = kv_ref[0, rows, LANES:2 * LANES].T
        vs_t[c] = ksvs_t[HEAD_DIM:2 * HEAD_DIM, :].astype(jnp.bfloat16)
        vw_t[c] = kwvw_t[HEAD_DIM:2 * HEAD_DIM, :].astype(jnp.bfloat16)
    for c in range(seq // TQ):
        rows = slice(c * TQ, (c + 1) * TQ)
        gate_t[c] = _sigmoid(misc_ref[0, rows, :] + gb_ref[...]).T

    k_pos0 = lax.broadcasted_iota(jnp.int32, (TK, TQ), 0)
    q_pos0 = lax.broadcasted_iota(jnp.int32, (TK, TQ), 1)
    blocks_per_tile = TK // SLC_LEN
    g_row0 = pl.multiple_of(g_id * GATE_GROUP_STRIDE, GATE_GROUP_STRIDE)

    def q_tile(qi, carry):
        t0 = pl.multiple_of(qi * TQ, TQ)
        gates = gate_t[qi, pl.ds(g_row0, GATE_GROUP_STRIDE), :]
        n_full = qi * (TQ // TK)
        outs = []
        for r in range(NSA_GROUP):
            q = qn_ref[0, r, pl.ds(t0, TQ), :]

            def slc_step(kj, state, causal):
                k0 = pl.multiple_of(kj * TK, TK)
                selb = jnp.concatenate(
                    [jnp.broadcast_to(sel_ref[0, 0, qi, kj * blocks_per_tile + i], (SLC_LEN, TQ))
                     for i in range(blocks_per_tile)], axis=0)
                keep = selb > 0.5
                if causal:
                    keep = keep & ((k0 + k_pos0) <= (t0 + q_pos0))
                return _flash_tile(q, ks_s[pl.ds(k0, TK), :], vs_t[kj], state, keep=keep)

            st = lax.fori_loop(0, n_full, lambda kj, s: slc_step(kj, s, False), _flash_init())
            for d in range(TQ // TK):
                st = slc_step(n_full + d, st, True)
            o_slc = _flash_out(st)

            def win_step(kj, state):
                k0 = pl.multiple_of(kj * TK, TK)
                diff = (t0 + q_pos0) - (k0 + k_pos0)
                keep = (diff >= 0) & (diff < WINDOW)
                return _flash_tile(q, kw_s[pl.ds(k0, TK), :], vw_t[kj], state, keep=keep)

            first = jnp.maximum(n_full - WINDOW // TK, 0)
            st = lax.fori_loop(first, n_full + TQ // TK, win_step, _flash_init())
            o_win = _flash_out(st)

            o_cmp = ocmp_ref[0, qi, r * HEAD_DIM:(r + 1) * HEAD_DIM, :]
            outs.append(gates[3 * r:3 * r + 1, :] * o_cmp + gates[3 * r + 1:3 * r + 2, :] * o_slc
                        + gates[3 * r + 2:3 * r + 3, :] * o_win)
        for pair in range(NSA_GROUP // 2):
            o2 = jnp.concatenate(outs[2 * pair:2 * pair + 2], axis=0).T
            lanes = slice(pair * LANES, (pair + 1) * LANES)
            zz = z_ref[0, pl.ds(t0, TQ), lanes]
            y_ref[0, pl.ds(t0, TQ), lanes] = (o2 * _silu(zz)).astype(y_ref.dtype)
        return carry

    lax.fori_loop(0, seq // TQ, q_tile, 0)


def _nsa_attend(qn, nkv, nz, ocmp_t, sel_t, misc, gate_b_row, k_g12):
    bsz, n_heads, seq, _ = qn.shape
    gw = NSA_GROUP * HEAD_DIM
    nq_t = seq // TQ
    n_slc = seq // SLC_LEN
    n_kt = seq // TK
    return pl.pallas_call(
        _nsa_kernel,
        out_shape=jax.ShapeDtypeStruct((bsz, seq, n_heads * HEAD_DIM), jnp.bfloat16),
        grid=(bsz, NSA_KV_HEADS),
        in_specs=[pl.BlockSpec((1, NSA_GROUP, seq, HEAD_DIM), lambda b, g: (b, g, 0, 0)),
                  pl.BlockSpec((1, seq, gw), lambda b, g: (b, 0, g)),
                  pl.BlockSpec((1, seq, gw), lambda b, g: (b, 0, g)),
                  pl.BlockSpec((1, nq_t, gw, TQ), lambda b, g: (b, 0, g, 0)),
                  pl.BlockSpec((1, 1, nq_t, n_slc, 1, TQ), lambda b, g: (b, g, 0, 0, 0, 0)),
                  pl.BlockSpec((1, seq, LANES), lambda b, g: (b, 0, 0)),
                  pl.BlockSpec((1, LANES), lambda b, g: (0, 0)),
                  pl.BlockSpec((2, HEAD_DIM), lambda b, g: (0, 0))],
        out_specs=pl.BlockSpec((1, seq, gw), lambda b, g: (b, 0, g)),
        scratch_shapes=[pltpu.VMEM((seq, HEAD_DIM), jnp.bfloat16),
                        pltpu.VMEM((seq, HEAD_DIM), jnp.bfloat16),
                        pltpu.VMEM((n_kt, HEAD_DIM, TK), jnp.bfloat16),
                        pltpu.VMEM((n_kt, HEAD_DIM, TK), jnp.bfloat16),
                        pltpu.VMEM((nq_t, LANES, TQ), jnp.float32)],
        compiler_params=_params(("parallel", "arbitrary")),
        name="nsa_attend",
    )(qn, nkv, nz, ocmp_t, sel_t, misc, gate_b_row, k_g12)


def _fox_kernel(q_ref, k_ref, v_ref, z_ref, misc_ref, pick_ref, fb_ref, qg_ref, kg_ref, y_ref,
                qn_s, kn_s, v_t, c_col, c_row):
    seq = q_ref.shape[1]
    n_kt = seq // TK
    heads = LANES // HEAD_DIM

    tri = (lax.broadcasted_iota(jnp.int32, (TK, TK), 0)
           >= lax.broadcasted_iota(jnp.int32, (TK, TK), 1)).astype(jnp.bfloat16)
    run = jnp.zeros((1, LANES), jnp.float32)
    for c in range(n_kt):
        rows = slice(c * TK, (c + 1) * TK)
        for hh in range(heads):
            lanes = slice(hh * HEAD_DIM, (hh + 1) * HEAD_DIM)
            qn_s[hh, rows, :] = (_head_rmsnorm(q_ref[0, rows, lanes], qg_ref[...])
                                 * ATTN_SCALE).astype(jnp.bfloat16)
            kn_s[hh, rows, :] = _head_rmsnorm(k_ref[0, rows, lanes], kg_ref[...]).astype(jnp.bfloat16)
        vt = v_ref[0, rows, :].T
        for hh in range(heads):
            v_t[hh, c] = vt[hh * HEAD_DIM:(hh + 1) * HEAD_DIM, :].astype(jnp.bfloat16)
        logits = _dot01_right(misc_ref[0, rows, :], pick_ref[0]) + fb_ref[0]
        log_f = jnp.minimum(logits, 0.0) - jnp.log1p(jnp.exp(-jnp.abs(logits)))
        cum = _dot01_left(tri, log_f) + run
        run = cum[TK - 1:TK, :]
        c_col[rows, :] = cum
    for c in range(seq // TQ):
        c_row[c] = c_col[c * TQ:(c + 1) * TQ, :].T

    k_pos0 = lax.broadcasted_iota(jnp.int32, (TK, TQ), 0)
    q_pos0 = lax.broadcasted_iota(jnp.int32, (TK, TQ), 1)

    def q_tile(qi, carry):
        t0 = pl.multiple_of(qi * TQ, TQ)
        n_full = qi * (TQ // TK)
        outs = []
        for hh in range(heads):
            q = qn_s[hh, pl.ds(t0, TQ), :]
            c_q = c_row[qi, hh:hh + 1, :]

            def step(kj, state, causal):
                k0 = pl.multiple_of(kj * TK, TK)
                bias = c_q - c_col[pl.ds(k0, TK), hh:hh + 1]
                keep = ((k0 + k_pos0) <= (t0 + q_pos0)) if causal else None
                return _flash_tile(q, kn_s[hh, pl.ds(k0, TK), :], v_t[hh, kj], state, bias=bias, keep=keep)

            st = lax.fori_loop(0, n_full, lambda kj, s: step(kj, s, False), _flash_init())
            for d in range(TQ // TK):
                st = step(n_full + d, st, True)
            outs.append(_flash_out(st))
        o2 = jnp.concatenate(outs, axis=0).T
        zz = z_ref[0, pl.ds(t0, TQ), :]
        y_ref[0, pl.ds(t0, TQ), :] = (o2 * _silu(zz)).astype(y_ref.dtype)
        return carry

    lax.fori_loop(0, seq // TQ, q_tile, 0)


def _fox_attend(fq, fk, fv, fz, misc, pick, fb_rows, q_g, k_g):
    bsz, seq, width = fq.shape
    n_pairs = width // LANES
    heads = LANES // HEAD_DIM
    blk = pl.BlockSpec((1, seq, LANES), lambda b, p: (b, 0, p))
    return pl.pallas_call(
        _fox_kernel,
        out_shape=jax.ShapeDtypeStruct((bsz, seq, width), jnp.bfloat16),
        grid=(bsz, n_pairs),
        in_specs=[blk, blk, blk, blk,
                  pl.BlockSpec((1, seq, LANES), lambda b, p: (b, 0, 0)),
                  pl.BlockSpec((1, LANES, LANES), lambda b, p: (p, 0, 0)),
                  pl.BlockSpec((1, 1, LANES), lambda b, p: (p, 0, 0)),
                  pl.BlockSpec((1, HEAD_DIM), lambda b, p: (0, 0)),
                  pl.BlockSpec((1, HEAD_DIM), lambda b, p: (0, 0))],
        out_specs=blk,
        scratch_shapes=[pltpu.VMEM((heads, seq, HEAD_DIM), jnp.bfloat16),
                        pltpu.VMEM((heads, seq, HEAD_DIM), jnp.bfloat16),
                        pltpu.VMEM((heads, seq // TK, HEAD_DIM, TK), jnp.bfloat16),
                        pltpu.VMEM((seq, LANES), jnp.float32),
                        pltpu.VMEM((seq // TQ, LANES, TQ), jnp.float32)],
        compiler_params=_params(("parallel", "arbitrary")),
        name="fox_attend",
    )(fq, fk, fv, fz, misc, pick, fb_rows, q_g.reshape(1, HEAD_DIM), k_g.reshape(1, HEAD_DIM))


def _block_diag(blocks):
    n, r, c = blocks.shape
    eye = jnp.eye(n, dtype=blocks.dtype)
    return (eye[:, None, :, None] * blocks[:, :, None, :]).reshape(n * r, n * c)


def _layer_layout(d_model):
    lru_w = d_model // 2
    n_heads = d_model // 128
    aw = n_heads * HEAD_DIM
    kvw = NSA_KV_HEADS * HEAD_DIM
    splits = (lru_w, lru_w, aw, kvw, kvw, kvw, kvw, kvw, kvw, 3 * n_heads, aw, aw, aw, aw, n_heads, aw)
    offs = np.concatenate([[0], np.cumsum(splits)])
    names = ("lru_u", "lru_z", "nq", "kc", "vc", "ks", "vs", "kw", "vw", "gl", "nz", "fq", "fk", "fv", "fl", "fz")
    col = {n: np.arange(offs[i], offs[i + 1]) for i, n in enumerate(names)}
    nkv = []
    for g in range(NSA_KV_HEADS):
        for n in ("ks", "vs", "kw", "vw"):
            nkv.append(col[n][g * HEAD_DIM:(g + 1) * HEAD_DIM])
    misc = np.full((LANES,), -1, np.int64)
    per_group = 3 * NSA_GROUP
    for g in range(NSA_KV_HEADS):
        misc[g * GATE_GROUP_STRIDE:g * GATE_GROUP_STRIDE + per_group] = col["gl"][g * per_group:(g + 1) * per_group]
    misc[FORGET_LANE0:FORGET_LANE0 + n_heads] = col["fl"]
    segs = [("lru", np.concatenate([col["lru_u"], col["lru_z"]])),
            ("nq", col["nq"]),
            ("kc", col["kc"]), ("vc", col["vc"]),
            ("nkv", np.concatenate(nkv)),
            ("nz", col["nz"]),
            ("fq", col["fq"]), ("fk", col["fk"]), ("fv", col["fv"]), ("fz", col["fz"]),
            ("misc", misc)]
    return segs, n_heads


def _permute_w_in(w_in, segs):
    d = w_in.shape[0]
    w_ext = jnp.concatenate([w_in, jnp.zeros((d, 1), w_in.dtype)], axis=1)
    idx = np.concatenate([np.where(c < 0, w_in.shape[1], c) for _, c in segs])
    return w_ext[:, idx].astype(jnp.bfloat16)


def _hybrid_layer(x, norm_g, w_in, w_out, conv_w, conv_b, lru_wa, lru_ba, lru_wx, lru_bx, lru_lambda,
                  nsa_q_g, nsa_k_g, cmp_pe_k, cmp_pe_v, cmp_wk, cmp_wv, nsa_gate_b,
                  fox_q_g, fox_k_g, fox_f_b):
    bsz, seq, d_model = x.shape
    segs, n_heads = _layer_layout(d_model)
    assert seq % TQ == 0 and seq // CMP_STRIDE == LANES and n_heads * 3 <= 2 * GATE_GROUP_STRIDE
    seg_widths = [len(c) for _, c in segs]
    x2d = x.reshape(bsz * seq, d_model)

    proj = _in_proj(x2d, norm_g, _permute_w_in(w_in, segs), seg_widths)
    p = {name: a.reshape(bsz, seq, a.shape[1]) for (name, _), a in zip(segs, proj)}

    w_gates = jnp.concatenate([_block_diag(lru_wa), _block_diag(lru_wx)], axis=1).astype(jnp.bfloat16)
    y_lru = _lru(p["lru"], conv_w, conv_b, w_gates, jnp.concatenate([lru_ba, lru_bx]), lru_lambda)

    tile2 = lambda a: jnp.tile(a, (1, NSA_KV_HEADS))
    bd2 = lambda w: jax.vmap(lambda m: _block_diag(jnp.stack([m] * NSA_KV_HEADS)))(w).astype(jnp.bfloat16)
    qn, ocmp_t, sel_t = _nsa_cmp(p["nq"], p["kc"], p["vc"], nsa_q_g, nsa_k_g[0], tile2(cmp_pe_k), tile2(cmp_pe_v),
                                 bd2(cmp_wk), bd2(cmp_wv))
    per_group = 3 * NSA_GROUP
    gate_b_row = jnp.zeros((1, LANES), jnp.float32)
    for g in range(NSA_KV_HEADS):
        gate_b_row = gate_b_row.at[0, g * GATE_GROUP_STRIDE:g * GATE_GROUP_STRIDE + per_group].set(
            nsa_gate_b[g * per_group:(g + 1) * per_group])
    y_nsa = _nsa_attend(qn, p["nkv"], p["nz"], ocmp_t, sel_t, p["misc"], gate_b_row, nsa_k_g[1:3])

    heads_per_blk = LANES // HEAD_DIM
    n_pairs = n_heads // heads_per_blk
    pick = np.zeros((n_pairs, LANES, LANES), np.float32)
    for pr in range(n_pairs):
        for hh in range(heads_per_blk):
            pick[pr, FORGET_LANE0 + pr * heads_per_blk + hh, hh] = 1.0
    fb_rows = jnp.zeros((n_pairs, 1, LANES), jnp.float32).at[:, 0, 0:heads_per_blk].set(
        fox_f_b.reshape(n_pairs, heads_per_blk))
    y_fox = _fox_attend(p["fq"], p["fk"], p["fv"], p["fz"], p["misc"], jnp.asarray(pick, jnp.bfloat16),
                        fb_rows, fox_q_g, fox_k_g)

    flat = lambda a: a.reshape(bsz * seq, a.shape[2])
    out = _out_proj(x2d, flat(y_lru), flat(y_nsa), flat(y_fox), w_out.astype(jnp.bfloat16))
    return out.reshape(bsz, seq, d_model)


def kernel(x, norm_g, w_in, w_out, conv_w, conv_b, lru_wa, lru_ba, lru_wx, lru_bx, lru_lambda, nsa_q_g, nsa_k_g, cmp_pe_k, cmp_pe_v, cmp_wk, cmp_wv, nsa_gate_b, fox_q_g, fox_k_g, fox_f_b):
    for l in range(norm_g.shape[0]):
        x = _hybrid_layer(x, norm_g[l], w_in[l], w_out[l], conv_w[l], conv_b[l], lru_wa[l], lru_ba[l],
                          lru_wx[l], lru_bx[l], lru_lambda[l], nsa_q_g[l], nsa_k_g[l], cmp_pe_k[l],
                          cmp_pe_v[l], cmp_wk[l], cmp_wv[l], nsa_gate_b[l], fox_q_g[l], fox_k_g[l],
                          fox_f_b[l])
    return x
```

```python
import functools

import jax
import jax.numpy as jnp
import numpy as np
from jax import lax
from jax.experimental import pallas as pl
from jax.experimental.pallas import tpu as pltpu

HEAD_DIM = 64
LRU_BLOCKS = 8
CONV_WIDTH = 4
LRU_C = 8.0
NSA_KV_HEADS = 2
NSA_GROUP = 4
CMP_LEN = 32
CMP_STRIDE = 16
SLC_LEN = 64
N_SELECT = 16
WINDOW = 512
NORM_EPS = 1e-6
NEG_INF = -1e30
FORCE_SCORE = 1e9
ATTN_SCALE = HEAD_DIM ** -0.5

LANES = 128
SUBLANES = 8
TQ = 256
TK = 128
KC = 512
PROJ_ROWS = 256
LRU_ROWS = 512
VMEM_LIMIT = 56 * 1024 * 1024

GATE_GROUP_STRIDE = 16
FORGET_LANE0 = 32
FOX_HEADS_PER_STEP = 4

_NT = (((1,), (1,)), ((), ()))


def _dot(a, b):
    return jnp.dot(a, b, preferred_element_type=jnp.float32)


def _dot_nt(a, b):
    return lax.dot_general(a, b, _NT, preferred_element_type=jnp.float32)


def _split3(x):
    hi = x.astype(jnp.bfloat16)
    r = x - hi.astype(jnp.float32)
    mid = r.astype(jnp.bfloat16)
    lo = (r - mid.astype(jnp.float32)).astype(jnp.bfloat16)
    return hi, mid, lo


def _dot01_left(m01, x):
    hi, mid, lo = _split3(x)
    return _dot(m01, hi) + _dot(m01, mid) + _dot(m01, lo)


def _dot01_right(x, m01):
    hi, mid, lo = _split3(x)
    return _dot(hi, m01) + _dot(mid, m01) + _dot(lo, m01)


def _head_rmsnorm(x, gain):
    ms = jnp.sum(x * x, axis=-1, keepdims=True) * (1.0 / HEAD_DIM)
    return x * lax.rsqrt(ms + NORM_EPS) * gain


def _sigmoid(x):
    return 1.0 / (1.0 + jnp.exp(-x))


def _silu(x):
    return x * _sigmoid(x)


def _params(sem):
    return pltpu.CompilerParams(dimension_semantics=sem, vmem_limit_bytes=VMEM_LIMIT)


def _in_proj_kernel(seg_bounds, x_ref, g_ref, w_ref, *out_refs):
    x = x_ref[...]
    ms = jnp.mean(x * x, axis=-1, keepdims=True)
    h = (x * lax.rsqrt(ms + NORM_EPS) * g_ref[...]).astype(jnp.bfloat16)
    for (lo, hi), o_ref in zip(seg_bounds, out_refs):
        o_ref[...] = _dot(h, w_ref[:, lo:hi]).astype(o_ref.dtype)


def _in_proj(x2d, norm_g, w_perm, seg_widths):
    n, d = x2d.shape
    bounds, off = [], 0
    for w in seg_widths:
        bounds.append((off, off + w))
        off += w
    assert off == w_perm.shape[1] and n % PROJ_ROWS == 0
    return pl.pallas_call(
        functools.partial(_in_proj_kernel, tuple(bounds)),
        out_shape=[jax.ShapeDtypeStruct((n, w), jnp.float32) for w in seg_widths],
        grid=(n // PROJ_ROWS,),
        in_specs=[pl.BlockSpec((PROJ_ROWS, d), lambda i: (i, 0)),
                  pl.BlockSpec((1, d), lambda i: (0, 0)),
                  pl.BlockSpec(w_perm.shape, lambda i: (0, 0))],
        out_specs=[pl.BlockSpec((PROJ_ROWS, w), lambda i: (i, 0)) for w in seg_widths],
        compiler_params=_params(("parallel",)),
        name="in_proj",
    )(x2d, norm_g.reshape(1, d), w_perm)


def _out_proj_kernel(x_ref, ya_ref, yb_ref, yc_ref, w_ref, o_ref):
    wa = ya_ref.shape[1]
    wb = yb_ref.shape[1]
    acc = x_ref[...]
    acc = acc + _dot(ya_ref[...], w_ref[0:wa, :])
    acc = acc + _dot(yb_ref[...], w_ref[wa:wa + wb, :])
    acc = acc + _dot(yc_ref[...], w_ref[wa + wb:, :])
    o_ref[...] = acc


def _out_proj(x2d, ya, yb, yc, w_out_bf16):
    n, d = x2d.shape
    row = lambda i: (i, 0)
    return pl.pallas_call(
        _out_proj_kernel,
        out_shape=jax.ShapeDtypeStruct((n, d), jnp.float32),
        grid=(n // PROJ_ROWS,),
        in_specs=[pl.BlockSpec((PROJ_ROWS, d), row),
                  pl.BlockSpec((PROJ_ROWS, ya.shape[1]), row),
                  pl.BlockSpec((PROJ_ROWS, yb.shape[1]), row),
                  pl.BlockSpec((PROJ_ROWS, yc.shape[1]), row),
                  pl.BlockSpec(w_out_bf16.shape, lambda i: (0, 0))],
        out_specs=pl.BlockSpec((PROJ_ROWS, d), row),
        compiler_params=_params(("parallel",)),
        name="out_proj",
    )(x2d, ya, yb, yc, w_out_bf16)


def _lru_kernel(uz_ref, cw_ref, cb_ref, wg_ref, bg_ref, lam_ref, y_ref, ubuf, a_s, b_s, hc):
    rows = a_s.shape[0]
    width = a_s.shape[1]

    @pl.when(pl.program_id(1) == 0)
    def _():
        ubuf[0:SUBLANES, :] = jnp.zeros((SUBLANES, width), jnp.float32)
        hc[...] = jnp.zeros_like(hc)

    u = uz_ref[0, :, 0:width]
    z = uz_ref[0, :, width:2 * width]
    ubuf[SUBLANES:SUBLANES + rows, :] = u
    xc = cb_ref[...] + cw_ref[CONV_WIDTH - 1:CONV_WIDTH, :] * u
    for k in range(CONV_WIDTH - 1):
        shift = CONV_WIDTH - 1 - k
        xc = xc + cw_ref[k:k + 1, :] * ubuf[SUBLANES - shift:SUBLANES - shift + rows, :]
    ubuf[0:SUBLANES, :] = u[rows - SUBLANES:rows, :]

    gates = _dot(xc.astype(jnp.bfloat16), wg_ref[...]) + bg_ref[...]
    r = _sigmoid(gates[:, 0:width])
    ig = _sigmoid(gates[:, width:2 * width])
    nlam = -lam_ref[...]
    softplus = jnp.maximum(nlam, 0.0) + jnp.log1p(jnp.exp(-jnp.abs(nlam)))
    log_a = (-LRU_C) * r * softplus
    a = jnp.exp(log_a)
    a_s[...] = a
    b_s[...] = jnp.sqrt(1.0 - a * a) * (ig * xc)

    row = lax.broadcasted_iota(jnp.int32, (SUBLANES, width), 0)

    def group(gi, h_prev):
        r0 = pl.multiple_of(gi * SUBLANES, SUBLANES)
        a8 = a_s[pl.ds(r0, SUBLANES), :]
        b8 = b_s[pl.ds(r0, SUBLANES), :]
        d = 1
        while d < SUBLANES:
            a_sh = jnp.where(row >= d, pltpu.roll(a8, d, 0), 1.0)
            b_sh = jnp.where(row >= d, pltpu.roll(b8, d, 0), 0.0)
            b8 = b8 + a8 * b_sh
            a8 = a8 * a_sh
            d *= 2
        h8 = b8 + a8 * h_prev
        b_s[pl.ds(r0, SUBLANES), :] = h8
        return h8[SUBLANES - 1:SUBLANES, :]

    hc[...] = lax.fori_loop(0, rows // SUBLANES, group, hc[...])
    y_ref[0] = (b_s[...] * _silu(z)).astype(y_ref.dtype)


def _lru(uz, conv_w, conv_b, w_gates, b_gates, lam):
    bsz, seq, w2 = uz.shape
    width = w2 // 2
    rows = min(LRU_ROWS, seq)
    assert seq % rows == 0
    const = lambda b, i: (0, 0)
    return pl.pallas_call(
        _lru_kernel,
        out_shape=jax.ShapeDtypeStruct((bsz, seq, width), jnp.bfloat16),
        grid=(bsz, seq // rows),
        in_specs=[pl.BlockSpec((1, rows, w2), lambda b, i: (b, i, 0)),
                  pl.BlockSpec(conv_w.shape, const),
                  pl.BlockSpec((1, width), const),
                  pl.BlockSpec(w_gates.shape, const),
                  pl.BlockSpec((1, w2), const),
                  pl.BlockSpec((1, width), const)],
        out_specs=pl.BlockSpec((1, rows, width), lambda b, i: (b, i, 0)),
        scratch_shapes=[pltpu.VMEM((rows + SUBLANES, width), jnp.float32),
                        pltpu.VMEM((rows, width), jnp.float32),
                        pltpu.VMEM((rows, width), jnp.float32),
                        pltpu.VMEM((1, width), jnp.float32)],
        compiler_params=_params(("parallel", "arbitrary")),
        name="rglru",
    )(uz, conv_w, conv_b.reshape(1, width), w_gates, b_gates.reshape(1, w2), lam.reshape(1, width))


def _cmp_kernel(q_ref, kc_ref, vc_ref, qg_ref, kg_ref, pek_ref, pev_ref, wk_ref, wv_ref,
                qaug_ref, ocmp_ref):
    seq = q_ref.shape[1]
    n_blk = seq // CMP_STRIDE
    n_cmp = (seq - CMP_LEN) // CMP_STRIDE + 1
    n_slc = seq // SLC_LEN
    half = CMP_LEN // CMP_STRIDE
    assert half == 2 and n_blk == LANES
    kvw = NSA_KV_HEADS * HEAD_DIM

    acc = [jnp.zeros((n_blk, kvw), jnp.float32) for _ in range(4)]
    for l in range(CMP_STRIDE):
        xk = kc_ref[0, pl.ds(l, n_blk, stride=CMP_STRIDE), :]
        xv = vc_ref[0, pl.ds(l, n_blk, stride=CMP_STRIDE), :]
        l2 = CMP_STRIDE + l
        acc[0] = acc[0] + _dot((xk + pek_ref[l:l + 1, :]).astype(jnp.bfloat16), wk_ref[l])
        acc[1] = acc[1] + _dot((xk + pek_ref[l2:l2 + 1, :]).astype(jnp.bfloat16), wk_ref[l2])
        acc[2] = acc[2] + _dot((xv + pev_ref[l:l + 1, :]).astype(jnp.bfloat16), wv_ref[l])
        acc[3] = acc[3] + _dot((xv + pev_ref[l2:l2 + 1, :]).astype(jnp.bfloat16), wv_ref[l2])
    k_cmp = acc[0] + pltpu.roll(acc[1], n_blk - 1, 0)
    v_cmp = acc[2] + pltpu.roll(acc[3], n_blk - 1, 0)
    v_cmp_t = v_cmp.T.astype(jnp.bfloat16)
    k_n = [_head_rmsnorm(k_cmp[:, g * HEAD_DIM:(g + 1) * HEAD_DIM], kg_ref[...]).astype(jnp.bfloat16)
           for g in range(NSA_KV_HEADS)]
    v_t = [v_cmp_t[g * HEAD_DIM:(g + 1) * HEAD_DIM, :] for g in range(NSA_KV_HEADS)]

    n_idx = lax.broadcasted_iota(jnp.int32, (n_blk, TQ), 0)
    j_idx = lax.broadcasted_iota(jnp.int32, (n_slc, TQ), 0)
    oj = lax.broadcasted_iota(jnp.int32, (n_slc, n_blk), 0) * SLC_LEN
    on = lax.broadcasted_iota(jnp.int32, (n_slc, n_blk), 1) * CMP_STRIDE
    overlap_t = ((on <= oj + SLC_LEN - 1) & (on + CMP_LEN - 1 >= oj)
                 & (on < n_cmp * CMP_STRIDE)).astype(jnp.bfloat16)

    def q_tile(qi, carry):
        t0 = pl.multiple_of(qi * TQ, TQ)
        t_row = t0 + lax.broadcasted_iota(jnp.int32, (1, TQ), 1)
        valid_cmp = (n_idx * CMP_STRIDE + CMP_LEN - 1 <= t_row) & (n_idx < n_cmp)
        blk_t = t_row // SLC_LEN
        valid_slc = j_idx <= blk_t
        forced = (j_idx == 0) | (j_idx == blk_t) | (j_idx == blk_t - 1)
        for g in range(NSA_KV_HEADS):
            p_sum = jnp.zeros((n_blk, TQ), jnp.float32)
            q_heads = []
            for r in range(NSA_GROUP):
                h = g * NSA_GROUP + r
                qh = q_ref[0, pl.ds(t0, TQ), h * HEAD_DIM:(h + 1) * HEAD_DIM]
                qh = (_head_rmsnorm(qh, qg_ref[...]) * ATTN_SCALE).astype(jnp.bfloat16)
                q_heads.append(qh)
                s = _dot_nt(k_n[g], qh)
                s = jnp.where(valid_cmp, s, NEG_INF)
                m = jnp.max(s, axis=0, keepdims=True)
                e = jnp.where(valid_cmp, jnp.exp(s - m), 0.0)
                p = e / jnp.maximum(jnp.sum(e, axis=0, keepdims=True), 1e-30)
                ocmp_ref[0, qi, h * HEAD_DIM:(h + 1) * HEAD_DIM, :] = _dot(v_t[g], p.astype(jnp.bfloat16))
                p_sum = p_sum + p
            imp = _dot01_left(overlap_t, p_sum)
            score = jnp.where(valid_slc, jnp.where(forced, FORCE_SCORE, imp), NEG_INF)
            rank = jnp.zeros((n_slc, TQ), jnp.float32)
            for jp in range(n_slc):
                other = score[jp:jp + 1, :]
                ahead = (other > score) | ((other == score) & (j_idx > jp))
                rank = rank + jnp.where(ahead, 1.0, 0.0)
            bias_t = jnp.where((rank < N_SELECT) & valid_slc, 0.0, NEG_INF)
            bias_t = jnp.concatenate([bias_t, jnp.zeros((LANES - n_slc, TQ), jnp.float32)], axis=0)
            bias = bias_t.T[:, 0:HEAD_DIM].astype(jnp.bfloat16)
            for r in range(NSA_GROUP):
                qaug_ref[0, g * NSA_GROUP + r, pl.ds(t0, TQ), :] = jnp.concatenate([q_heads[r], bias], axis=-1)
        return carry

    lax.fori_loop(0, seq // TQ, q_tile, 0)


def _nsa_cmp(nq, kc, vc, q_g, k_g0, pe_k2, pe_v2, wk_bd, wv_bd):
    bsz, seq, qw = nq.shape
    n_heads = qw // HEAD_DIM
    nq_t = seq // TQ
    assert seq // SLC_LEN <= HEAD_DIM
    c2 = lambda b: (0, 0)
    c3 = lambda b: (0, 0, 0)
    return pl.pallas_call(
        _cmp_kernel,
        out_shape=[jax.ShapeDtypeStruct((bsz, n_heads, seq, LANES), jnp.bfloat16),
                   jax.ShapeDtypeStruct((bsz, nq_t, qw, TQ), jnp.float32)],
        grid=(bsz,),
        in_specs=[pl.BlockSpec((1, seq, qw), lambda b: (b, 0, 0)),
                  pl.BlockSpec((1, seq, kc.shape[2]), lambda b: (b, 0, 0)),
                  pl.BlockSpec((1, seq, vc.shape[2]), lambda b: (b, 0, 0)),
                  pl.BlockSpec((1, HEAD_DIM), c2),
                  pl.BlockSpec((1, HEAD_DIM), c2),
                  pl.BlockSpec(pe_k2.shape, c2),
                  pl.BlockSpec(pe_v2.shape, c2),
                  pl.BlockSpec(wk_bd.shape, c3),
                  pl.BlockSpec(wv_bd.shape, c3)],
        out_specs=[pl.BlockSpec((1, n_heads, seq, LANES), lambda b: (b, 0, 0, 0)),
                   pl.BlockSpec((1, nq_t, qw, TQ), lambda b: (b, 0, 0, 0))],
        compiler_params=_params(("parallel",)),
        name="nsa_compress",
    )(nq, kc, vc, q_g.reshape(1, HEAD_DIM), k_g0.reshape(1, HEAD_DIM), pe_k2, pe_v2, wk_bd, wv_bd)


def _flash_chunk(q, k, v_t, state, keep=None):
    return _flash_chunks([q], [k], [v_t], [state], keep)[0]


def _flash_chunks(qs, ks, v_ts, states, keep=None):
    heads = range(len(qs))
    ss = [_dot_nt(ks[h], qs[h]) for h in heads]
    if keep is not None:
        ss = [jnp.where(keep, s, NEG_INF) for s in ss]
    m_new = [jnp.maximum(states[h][0], jnp.max(ss[h], axis=0, keepdims=True)) for h in heads]
    ps = [jnp.exp(ss[h] - m_new[h]) for h in heads]
    out = []
    for h in heads:
        m, l, acc = states[h]
        alpha = jnp.exp(m - m_new[h])
        l = alpha * l + jnp.sum(ps[h], axis=0, keepdims=True)
        acc = alpha * acc + _dot(v_ts[h], ps[h].astype(jnp.bfloat16))
        out.append((m_new[h], l, acc))
    return tuple(out)


def _flash_init():
    return (jnp.full((1, TQ), NEG_INF, jnp.float32), jnp.zeros((1, TQ), jnp.float32),
            jnp.zeros((HEAD_DIM, TQ), jnp.float32))


def _flash_out(state):
    _, l, acc = state
    return acc / jnp.maximum(l, 1e-30)


def _causal_flash(qs, k_of, vt_of, qi):
    heads = range(len(qs))
    t0 = qi * TQ
    k_pos = lax.broadcasted_iota(jnp.int32, (KC, TQ), 0)
    q_pos = lax.broadcasted_iota(jnp.int32, (KC, TQ), 1)

    def chunk(kc, states, masked):
        k0 = pl.multiple_of(kc * KC, KC)
        keep = ((k0 + k_pos) <= (t0 + q_pos)) if masked else None
        return _flash_chunks(qs, [k_of(h, k0) for h in heads], [vt_of(h, kc) for h in heads], states, keep)

    n_full = qi // (KC // TQ)
    st = lax.fori_loop(0, n_full, lambda kc, s: chunk(kc, s, False), tuple(_flash_init() for _ in heads))
    st = chunk(n_full, st, True)
    return [_flash_out(s) for s in st]


def _nsa_kernel(qaug_ref, kv_ref, z_ref, ocmp_ref, misc_ref, gb_ref, kg_ref, y_ref,
                ks_s, kw_s, vs_t, vw_t, gate_t):
    seq = kv_ref.shape[1]
    g_id = pl.program_id(1)

    lane = lax.broadcasted_iota(jnp.int32, (TQ, HEAD_DIM), 1)
    row = lax.broadcasted_iota(jnp.int32, (TQ, HEAD_DIM), 0)
    zeros = jnp.zeros((TQ, HEAD_DIM), jnp.float32)
    for c in range(seq // TQ):
        rows = slice(c * TQ, (c + 1) * TQ)
        ks = _head_rmsnorm(kv_ref[0, rows, 0:HEAD_DIM], kg_ref[0:1, :])
        block_onehot = jnp.where(lane == (c * TQ + row) // SLC_LEN, 1.0, 0.0)
        ks_s[rows, :] = jnp.concatenate([ks, block_onehot], axis=-1).astype(jnp.bfloat16)
        kw = _head_rmsnorm(kv_ref[0, rows, 2 * HEAD_DIM:3 * HEAD_DIM], kg_ref[1:2, :])
        kw_s[rows, :] = jnp.concatenate([kw, zeros], axis=-1).astype(jnp.bfloat16)
        ksvs_t = kv_ref[0, rows, 0:LANES].T
        kwvw_t = kv_ref[0, rows, LANES:2 * LANES].T
        off = (c % (KC // TQ)) * TQ
        vs_t[c // (KC // TQ), :, off:off + TQ] = ksvs_t[HEAD_DIM:2 * HEAD_DIM, :].astype(jnp.bfloat16)
        vw_t[c] = kwvw_t[HEAD_DIM:2 * HEAD_DIM, :].astype(jnp.bfloat16)
        gate_t[c] = _sigmoid(misc_ref[0, rows, :] + gb_ref[...]).T

    k_pos = lax.broadcasted_iota(jnp.int32, (TQ, TQ), 0)
    q_pos = lax.broadcasted_iota(jnp.int32, (TQ, TQ), 1)
    g_row0 = pl.multiple_of(g_id * GATE_GROUP_STRIDE, GATE_GROUP_STRIDE)
    heads = range(NSA_GROUP)

    def q_tile(qi, carry):
        t0 = pl.multiple_of(qi * TQ, TQ)
        gates = gate_t[qi, pl.ds(g_row0, GATE_GROUP_STRIDE), :]
        qs = [qaug_ref[0, r, pl.ds(t0, TQ), :] for r in heads]

        o_slc = _causal_flash(qs, lambda h, k0: ks_s[pl.ds(k0, KC), :], lambda h, kc: vs_t[kc], qi)

        def win_chunk(kc, states):
            k0 = pl.multiple_of(kc * TQ, TQ)
            diff = (t0 + q_pos) - (k0 + k_pos)
            keep = (diff >= 0) & (diff < WINDOW)
            k = kw_s[pl.ds(k0, TQ), :]
            return _flash_chunks(qs, [k] * NSA_GROUP, [vw_t[kc]] * NSA_GROUP, states, keep)

        st = lax.fori_loop(jnp.maximum(qi - WINDOW // TQ, 0), qi + 1, win_chunk,
                           tuple(_flash_init() for _ in heads))
        o_win = [_flash_out(s) for s in st]

        outs = []
        for r in heads:
            o_cmp = ocmp_ref[0, qi, r * HEAD_DIM:(r + 1) * HEAD_DIM, :]
            outs.append(gates[3 * r:3 * r + 1, :] * o_cmp + gates[3 * r + 1:3 * r + 2, :] * o_slc[r]
                        + gates[3 * r + 2:3 * r + 3, :] * o_win[r])
        for pair in range(NSA_GROUP // 2):
            o2 = jnp.concatenate(outs[2 * pair:2 * pair + 2], axis=0).T
            lanes = slice(pair * LANES, (pair + 1) * LANES)
            zz = z_ref[0, pl.ds(t0, TQ), lanes]
            y_ref[0, pl.ds(t0, TQ), lanes] = (o2 * _silu(zz)).astype(y_ref.dtype)
        return carry

    lax.fori_loop(0, seq // TQ, q_tile, 0)


def _nsa_attend(qaug, nkv, nz, ocmp_t, misc, gate_b_row, k_g12):
    bsz, n_heads, seq, _ = qaug.shape
    gw = NSA_GROUP * HEAD_DIM
    nq_t = seq // TQ
    assert seq % KC == 0 and WINDOW % TQ == 0
    return pl.pallas_call(
        _nsa_kernel,
        out_shape=jax.ShapeDtypeStruct((bsz, seq, n_heads * HEAD_DIM), jnp.bfloat16),
        grid=(bsz, NSA_KV_HEADS),
        in_specs=[pl.BlockSpec((1, NSA_GROUP, seq, LANES), lambda b, g: (b, g, 0, 0)),
                  pl.BlockSpec((1, seq, gw), lambda b, g: (b, 0, g)),
                  pl.BlockSpec((1, seq, gw), lambda b, g: (b, 0, g)),
                  pl.BlockSpec((1, nq_t, gw, TQ), lambda b, g: (b, 0, g, 0)),
                  pl.BlockSpec((1, seq, LANES), lambda b, g: (b, 0, 0)),
                  pl.BlockSpec((1, LANES), lambda b, g: (0, 0)),
                  pl.BlockSpec((2, HEAD_DIM), lambda b, g: (0, 0))],
        out_specs=pl.BlockSpec((1, seq, gw), lambda b, g: (b, 0, g)),
        scratch_shapes=[pltpu.VMEM((seq, LANES), jnp.bfloat16),
                        pltpu.VMEM((seq, LANES), jnp.bfloat16),
                        pltpu.VMEM((seq // KC, HEAD_DIM, KC), jnp.bfloat16),
                        pltpu.VMEM((nq_t, HEAD_DIM, TQ), jnp.bfloat16),
                        pltpu.VMEM((nq_t, LANES, TQ), jnp.float32)],
        compiler_params=_params(("parallel", "arbitrary")),
        name="nsa_attend",
    )(qaug, nkv, nz, ocmp_t, misc, gate_b_row, k_g12)


def _fox_kernel(q_ref, k_ref, v_ref, z_ref, misc_ref, pick_ref, fb_ref, qg_ref, kg_ref, y_ref,
                qa_s, ka_s, v_t):
    seq = q_ref.shape[1]
    n_heads = q_ref.shape[2] // HEAD_DIM

    tri = (lax.broadcasted_iota(jnp.int32, (TK, TK), 0)
           >= lax.broadcasted_iota(jnp.int32, (TK, TK), 1)).astype(jnp.bfloat16)
    lane = lax.broadcasted_iota(jnp.int32, (TK, HEAD_DIM), 1)
    ones3 = jnp.where(lane < 3, 1.0, 0.0)
    run = jnp.zeros((1, LANES), jnp.float32)
    for c in range(seq // TK):
        rows = slice(c * TK, (c + 1) * TK)
        logits = _dot01_right(misc_ref[0, rows, :], pick_ref[0]) + fb_ref[0]
        log_f = jnp.minimum(logits, 0.0) - jnp.log1p(jnp.exp(-jnp.abs(logits)))
        cum = _dot01_left(tri, log_f) + run
        run = cum[TK - 1:TK, :]
        for hh in range(n_heads):
            lanes = slice(hh * HEAD_DIM, (hh + 1) * HEAD_DIM)
            qn = _head_rmsnorm(q_ref[0, rows, lanes], qg_ref[...]) * ATTN_SCALE
            qa_s[hh, rows, :] = jnp.concatenate([qn, ones3], axis=-1).astype(jnp.bfloat16)
            kn = _head_rmsnorm(k_ref[0, rows, lanes], kg_ref[...])
            hi, mid, lo = (t.astype(jnp.float32) for t in _split3(-cum[:, hh:hh + 1]))
            extra = jnp.where(lane == 0, hi, jnp.where(lane == 1, mid, jnp.where(lane == 2, lo, 0.0)))
            ka_s[hh, rows, :] = jnp.concatenate([kn, extra], axis=-1).astype(jnp.bfloat16)
        off = (c % (KC // TK)) * TK
        for blk in range(n_heads * HEAD_DIM // LANES):
            vt = v_ref[0, rows, blk * LANES:(blk + 1) * LANES].T
            for j in range(LANES // HEAD_DIM):
                hh = blk * (LANES // HEAD_DIM) + j
                v_t[hh, c // (KC // TK), :, off:off + TK] = vt[j * HEAD_DIM:(j + 1) * HEAD_DIM, :].astype(jnp.bfloat16)

    def q_tile(qi, carry):
        t0 = pl.multiple_of(qi * TQ, TQ)
        qs = [qa_s[hh, pl.ds(t0, TQ), :] for hh in range(n_heads)]
        outs = _causal_flash(qs, lambda h, k0: ka_s[h, pl.ds(k0, KC), :], lambda h, kc: v_t[h, kc], qi)
        for blk in range(n_heads * HEAD_DIM // LANES):
            o2 = jnp.concatenate(outs[2 * blk:2 * blk + 2], axis=0).T
            lanes = slice(blk * LANES, (blk + 1) * LANES)
            zz = z_ref[0, pl.ds(t0, TQ), lanes]
            y_ref[0, pl.ds(t0, TQ), lanes] = (o2 * _silu(zz)).astype(y_ref.dtype)
        return carry

    lax.fori_loop(0, seq // TQ, q_tile, 0)


def _fox_attend(fq, fk, fv, fz, misc, pick, fb_rows, q_g, k_g):
    bsz, seq, width = fq.shape
    n_steps = pick.shape[0]
    bw = width // n_steps
    heads = bw // HEAD_DIM
    assert seq % KC == 0
    blk = pl.BlockSpec((1, seq, bw), lambda b, p: (b, 0, p))
    return pl.pallas_call(
        _fox_kernel,
        out_shape=jax.ShapeDtypeStruct((bsz, seq, width), jnp.bfloat16),
        grid=(bsz, n_steps),
        in_specs=[blk, blk, blk, blk,
                  pl.BlockSpec((1, seq, LANES), lambda b, p: (b, 0, 0)),
                  pl.BlockSpec((1, LANES, LANES), lambda b, p: (p, 0, 0)),
                  pl.BlockSpec((1, 1, LANES), lambda b, p: (p, 0, 0)),
                  pl.BlockSpec((1, HEAD_DIM), lambda b, p: (0, 0)),
                  pl.BlockSpec((1, HEAD_DIM), lambda b, p: (0, 0))],
        out_specs=blk,
        scratch_shapes=[pltpu.VMEM((heads, seq, LANES), jnp.bfloat16),
                        pltpu.VMEM((heads, seq, LANES), jnp.bfloat16),
                        pltpu.VMEM((heads, seq // KC, HEAD_DIM, KC), jnp.bfloat16)],
        compiler_params=_params(("parallel", "arbitrary")),
        name="fox_attend",
    )(fq, fk, fv, fz, misc, pick, fb_rows, q_g.reshape(1, HEAD_DIM), k_g.reshape(1, HEAD_DIM))


def _block_diag(blocks):
    n, r, c = blocks.shape
    eye = jnp.eye(n, dtype=blocks.dtype)
    return (eye[:, None, :, None] * blocks[:, :, None, :]).reshape(n * r, n * c)


def _layer_layout(d_model):
    lru_w = d_model // 2
    n_heads = d_model // 128
    aw = n_heads * HEAD_DIM
    kvw = NSA_KV_HEADS * HEAD_DIM
    splits = (lru_w, lru_w, aw, kvw, kvw, kvw, kvw, kvw, kvw, 3 * n_heads, aw, aw, aw, aw, n_heads, aw)
    offs = np.concatenate([[0], np.cumsum(splits)])
    names = ("lru_u", "lru_z", "nq", "kc", "vc", "ks", "vs", "kw", "vw", "gl", "nz", "fq", "fk", "fv", "fl", "fz")
    col = {n: np.arange(offs[i], offs[i + 1]) for i, n in enumerate(names)}
    nkv = []
    for g in range(NSA_KV_HEADS):
        for n in ("ks", "vs", "kw", "vw"):
            nkv.append(col[n][g * HEAD_DIM:(g + 1) * HEAD_DIM])
    misc = np.full((LANES,), -1, np.int64)
    per_group = 3 * NSA_GROUP
    for g in range(NSA_KV_HEADS):
        misc[g * GATE_GROUP_STRIDE:g * GATE_GROUP_STRIDE + per_group] = col["gl"][g * per_group:(g + 1) * per_group]
    misc[FORGET_LANE0:FORGET_LANE0 + n_heads] = col["fl"]
    segs = [("lru", np.concatenate([col["lru_u"], col["lru_z"]])),
            ("nq", col["nq"]),
            ("kc", col["kc"]), ("vc", col["vc"]),
            ("nkv", np.concatenate(nkv)),
            ("nz", col["nz"]),
            ("fq", col["fq"]), ("fk", col["fk"]), ("fv", col["fv"]), ("fz", col["fz"]),
            ("misc", misc)]
    return segs, n_heads


def _permute_w_in(w_in, segs):
    d = w_in.shape[0]
    w_ext = jnp.concatenate([w_in, jnp.zeros((d, 1), w_in.dtype)], axis=1)
    idx = np.concatenate([np.where(c < 0, w_in.shape[1], c) for _, c in segs])
    return w_ext[:, idx].astype(jnp.bfloat16)


def _hybrid_layer(x, norm_g, w_in, w_out, conv_w, conv_b, lru_wa, lru_ba, lru_wx, lru_bx, lru_lambda,
                  nsa_q_g, nsa_k_g, cmp_pe_k, cmp_pe_v, cmp_wk, cmp_wv, nsa_gate_b,
                  fox_q_g, fox_k_g, fox_f_b):
    bsz, seq, d_model = x.shape
    segs, n_heads = _layer_layout(d_model)
    assert seq % TQ == 0 and seq // CMP_STRIDE == LANES and n_heads * 3 <= 2 * GATE_GROUP_STRIDE
    seg_widths = [len(c) for _, c in segs]
    x2d = x.reshape(bsz * seq, d_model)

    proj = _in_proj(x2d, norm_g, _permute_w_in(w_in, segs), seg_widths)
    p = {name: a.reshape(bsz, seq, a.shape[1]) for (name, _), a in zip(segs, proj)}

    w_gates = jnp.concatenate([_block_diag(lru_wa), _block_diag(lru_wx)], axis=1).astype(jnp.bfloat16)
    y_lru = _lru(p["lru"], conv_w, conv_b, w_gates, jnp.concatenate([lru_ba, lru_bx]), lru_lambda)

    tile2 = lambda a: jnp.tile(a, (1, NSA_KV_HEADS))
    bd2 = lambda w: jax.vmap(lambda m: _block_diag(jnp.stack([m] * NSA_KV_HEADS)))(w).astype(jnp.bfloat16)
    qaug, ocmp_t = _nsa_cmp(p["nq"], p["kc"], p["vc"], nsa_q_g, nsa_k_g[0], tile2(cmp_pe_k), tile2(cmp_pe_v),
                                 bd2(cmp_wk), bd2(cmp_wv))
    per_group = 3 * NSA_GROUP
    gate_b_row = jnp.zeros((1, LANES), jnp.float32)
    for g in range(NSA_KV_HEADS):
        gate_b_row = gate_b_row.at[0, g * GATE_GROUP_STRIDE:g * GATE_GROUP_STRIDE + per_group].set(
            nsa_gate_b[g * per_group:(g + 1) * per_group])
    y_nsa = _nsa_attend(qaug, p["nkv"], p["nz"], ocmp_t, p["misc"], gate_b_row, nsa_k_g[1:3])

    n_steps = n_heads // FOX_HEADS_PER_STEP
    pick = np.zeros((n_steps, LANES, LANES), np.float32)
    for st in range(n_steps):
        for hh in range(FOX_HEADS_PER_STEP):
            pick[st, FORGET_LANE0 + st * FOX_HEADS_PER_STEP + hh, hh] = 1.0
    fb_rows = jnp.zeros((n_steps, 1, LANES), jnp.float32).at[:, 0, 0:FOX_HEADS_PER_STEP].set(
        fox_f_b.reshape(n_steps, FOX_HEADS_PER_STEP))
    y_fox = _fox_attend(p["fq"], p["fk"], p["fv"], p["fz"], p["misc"], jnp.asarray(pick, jnp.bfloat16),
                        fb_rows, fox_q_g, fox_k_g)

    flat = lambda a: a.reshape(bsz * seq, a.shape[2])
    out = _out_proj(x2d, flat(y_lru), flat(y_nsa), flat(y_fox), w_out.astype(jnp.bfloat16))
    return out.reshape(bsz, seq, d_model)


def kernel(x, norm_g, w_in, w_out, conv_w, conv_b, lru_wa, lru_ba, lru_wx, lru_bx, lru_lambda, nsa_q_g, nsa_k_g, cmp_pe_k, cmp_pe_v, cmp_wk, cmp_wv, nsa_gate_b, fox_q_g, fox_k_g, fox_f_b):
    for l in range(norm_g.shape[0]):
        x = _hybrid_layer(x, norm_g[l], w_in[l], w_out[l], conv_w[l], conv_b[l], lru_wa[l], lru_ba[l],
                          lru_wx[l], lru_bx[l], lru_lambda[l], nsa_q_g[l], nsa_k_g[l], cmp_pe_k[l],
                          cmp_pe_v[l], cmp_wk[l], cmp_wv[l], nsa_gate_b[l], fox_q_g[l], fox_k_g[l],
                          fox_f_b[l])
    return x
```

```python
import functools

import jax
import jax.numpy as jnp
import numpy as np
from jax import lax
from jax.experimental import pallas as pl
from jax.experimental.pallas import tpu as pltpu

HEAD_DIM = 64
LRU_BLOCKS = 8
CONV_WIDTH = 4
LRU_C = 8.0
NSA_KV_HEADS = 2
NSA_GROUP = 4
CMP_LEN = 32
CMP_STRIDE = 16
SLC_LEN = 64
N_SELECT = 16
WINDOW = 512
NORM_EPS = 1e-6
NEG_INF = -1e30
FORCE_SCORE = 1e9
ATTN_SCALE = HEAD_DIM ** -0.5

LANES = 128
SUBLANES = 8
TQ = 256
KC = 512
PROJ_ROWS = TQ
LRU_ROWS = 512
VMEM_LIMIT = 56 * 1024 * 1024

GATE_GROUP_STRIDE = 16
FORGET_LANE0 = 32
FOX_HEADS_PER_STEP = 4
BIAS_TERMS = 3

_NT = (((1,), (1,)), ((), ()))


def _dot(a, b):
    return jnp.dot(a, b, preferred_element_type=jnp.float32)


def _dot_nt(a, b):
    return lax.dot_general(a, b, _NT, preferred_element_type=jnp.float32)


def _split3(x):
    hi = x.astype(jnp.bfloat16)
    r = x - hi.astype(jnp.float32)
    mid = r.astype(jnp.bfloat16)
    lo = (r - mid.astype(jnp.float32)).astype(jnp.bfloat16)
    return hi, mid, lo


def _dot01_left(m01, x):
    hi, mid, lo = _split3(x)
    return _dot(m01, hi) + _dot(m01, mid) + _dot(m01, lo)


def _head_rmsnorm(x, gain):
    ms = jnp.sum(x * x, axis=-1, keepdims=True) * (1.0 / HEAD_DIM)
    return x * lax.rsqrt(ms + NORM_EPS) * gain


def _heads_rmsnorm(x, ones_bd, gain_row):
    sq = x * x
    hi = sq.astype(jnp.bfloat16)
    lo = (sq - hi.astype(jnp.float32)).astype(jnp.bfloat16)
    ss = _dot(hi, ones_bd) + _dot(lo, ones_bd)
    return x * lax.rsqrt(ss * (1.0 / HEAD_DIM) + NORM_EPS) * gain_row


def _sigmoid(x):
    return 1.0 / (1.0 + jnp.exp(-x))


def _silu(x):
    return x * _sigmoid(x)


def _params(sem):
    return pltpu.CompilerParams(dimension_semantics=sem, vmem_limit_bytes=VMEM_LIMIT)


def _in_proj_kernel(seg, tiles_per_seq, x_ref, g_ref, w_ref, bd_ref, gain_ref, mb_ref, place_ref,
                    lru_ref, nqn_ref, kc_ref, vc_ref, nks_ref, nkw_ref, nvt_ref, ngt_ref, nzg_ref,
                    fqn_ref, fkn_ref, fex_ref, fvt_ref, fzg_ref, carry):
    i = pl.program_id(0)
    rows = x_ref.shape[0]
    x = x_ref[...]
    ms = jnp.mean(x * x, axis=-1, keepdims=True)
    h = (x * lax.rsqrt(ms + NORM_EPS) * g_ref[...]).astype(jnp.bfloat16)
    proj = lambda name: _dot(h, w_ref[:, seg[name][0]:seg[name][1]])
    bd = bd_ref[...]
    bf = jnp.bfloat16

    lru_ref[...] = proj("lru")
    kc_ref[...] = proj("kc")
    vc_ref[...] = proj("vc")
    nqn_ref[...] = _heads_rmsnorm(proj("nq"), bd, gain_ref[0:1, :]).astype(bf)
    nzg_ref[...] = _silu(proj("nz")).astype(bf)
    fqn_ref[...] = _heads_rmsnorm(proj("fq"), bd, gain_ref[2:3, :]).astype(bf)
    fkn_ref[...] = _heads_rmsnorm(proj("fk"), bd, gain_ref[3:4, :]).astype(bf)
    fzg_ref[...] = _silu(proj("fz")).astype(bf)
    fv = proj("fv")
    for j in range(fv.shape[1] // LANES):
        fvt_ref[0, j * LANES:(j + 1) * LANES, :] = fv[:, j * LANES:(j + 1) * LANES].T.astype(bf)

    nkv = proj("nkv")
    nkv_n = _heads_rmsnorm(nkv, bd, gain_ref[1:2, :])
    lane = lax.broadcasted_iota(jnp.int32, (rows, LANES), 1)
    pos = (i % tiles_per_seq) * rows + lax.broadcasted_iota(jnp.int32, (rows, LANES), 0)
    onehot = jnp.where(lane - HEAD_DIM == pos // SLC_LEN, 1.0, 0.0)
    for g in range(NSA_KV_HEADS):
        b0 = 2 * g * LANES
        nks_ref[:, g * LANES:(g + 1) * LANES] = jnp.where(lane < HEAD_DIM, nkv_n[:, b0:b0 + LANES], onehot).astype(bf)
        nkw_ref[:, g * LANES:(g + 1) * LANES] = jnp.where(lane < HEAD_DIM, nkv_n[:, b0 + LANES:b0 + 2 * LANES],
                                                          0.0).astype(bf)
    for j in range(nkv.shape[1] // LANES):
        nvt_ref[0, j * LANES:(j + 1) * LANES, :] = nkv[:, j * LANES:(j + 1) * LANES].T.astype(bf)

    misc = proj("misc") + mb_ref[...]
    ngt_ref[0] = _sigmoid(misc).T
    log_f = jnp.minimum(misc, 0.0) - jnp.log1p(jnp.exp(-jnp.abs(misc)))

    @pl.when(i % tiles_per_seq == 0)
    def _():
        carry[...] = jnp.zeros_like(carry)

    tri = (lax.broadcasted_iota(jnp.int32, (rows, rows), 0)
           >= lax.broadcasted_iota(jnp.int32, (rows, rows), 1)).astype(bf)
    cum = _dot01_left(tri, log_f) + carry[...]
    carry[...] = cum[rows - 1:rows, :]
    terms = _split3(-cum)
    fex = _dot(terms[0], place_ref[0])
    for t in range(1, BIAS_TERMS):
        fex = fex + _dot(terms[t], place_ref[t])
    fex_ref[...] = fex.astype(bf)


def _in_proj(x2d, norm_g, w_perm, seg, seq, ones_bd, gain_rows, misc_bias, place):
    n, d = x2d.shape
    aw = seg["nq"][1] - seg["nq"][0]
    kvw = seg["nkv"][1] - seg["nkv"][0]
    rows = PROJ_ROWS
    assert n % rows == 0 and seq % rows == 0
    n_tiles = n // rows
    f32, bf = jnp.float32, jnp.bfloat16
    row_out = lambda w, dt: (jax.ShapeDtypeStruct((n, w), dt), pl.BlockSpec((rows, w), lambda i: (i, 0)))
    t_out = lambda w, dt: (jax.ShapeDtypeStruct((n_tiles, w, rows), dt), pl.BlockSpec((1, w, rows), lambda i: (i, 0, 0)))
    outs = [row_out(seg["lru"][1] - seg["lru"][0], f32),
            row_out(aw, bf),
            row_out(LANES, f32), row_out(LANES, f32),
            row_out(NSA_KV_HEADS * LANES, bf),
            row_out(NSA_KV_HEADS * LANES, bf),
            t_out(kvw, bf),
            t_out(LANES, f32),
            row_out(aw, bf),
            row_out(aw, bf), row_out(aw, bf), row_out(aw, bf),
            t_out(aw, bf),
            row_out(aw, bf)]
    const2 = lambda a: pl.BlockSpec(a.shape, lambda i: (0, 0))
    return pl.pallas_call(
        functools.partial(_in_proj_kernel, seg, seq // rows),
        out_shape=[o[0] for o in outs],
        grid=(n_tiles,),
        in_specs=[pl.BlockSpec((rows, d), lambda i: (i, 0)),
                  pl.BlockSpec((1, d), lambda i: (0, 0)),
                  const2(w_perm), const2(ones_bd), const2(gain_rows), const2(misc_bias),
                  pl.BlockSpec(place.shape, lambda i: (0, 0, 0))],
        out_specs=[o[1] for o in outs],
        scratch_shapes=[pltpu.VMEM((1, LANES), jnp.float32)],
        compiler_params=_params(("arbitrary",)),
        name="in_proj",
    )(x2d, norm_g.reshape(1, d), w_perm, ones_bd, gain_rows, misc_bias, place)


def _out_proj_kernel(x_ref, ya_ref, yb_ref, yc_ref, w_ref, o_ref):
    wa = ya_ref.shape[1]
    wb = yb_ref.shape[1]
    acc = x_ref[...]
    acc = acc + _dot(ya_ref[...], w_ref[0:wa, :])
    acc = acc + _dot(yb_ref[...], w_ref[wa:wa + wb, :])
    acc = acc + _dot(yc_ref[...], w_ref[wa + wb:, :])
    o_ref[...] = acc


def _out_proj(x2d, ya, yb, yc, w_out_bf16):
    n, d = x2d.shape
    row = lambda i: (i, 0)
    return pl.pallas_call(
        _out_proj_kernel,
        out_shape=jax.ShapeDtypeStruct((n, d), jnp.float32),
        grid=(n // PROJ_ROWS,),
        in_specs=[pl.BlockSpec((PROJ_ROWS, d), row),
                  pl.BlockSpec((PROJ_ROWS, ya.shape[1]), row),
                  pl.BlockSpec((PROJ_ROWS, yb.shape[1]), row),
                  pl.BlockSpec((PROJ_ROWS, yc.shape[1]), row),
                  pl.BlockSpec(w_out_bf16.shape, lambda i: (0, 0))],
        out_specs=pl.BlockSpec((PROJ_ROWS, d), row),
        compiler_params=_params(("parallel",)),
        name="out_proj",
    )(x2d, ya, yb, yc, w_out_bf16)


def _lru_kernel(uz_ref, cw_ref, cb_ref, wg_ref, bg_ref, lam_ref, y_ref, ubuf, a_s, b_s, hc):
    rows = a_s.shape[0]
    width = a_s.shape[1]

    @pl.when(pl.program_id(1) == 0)
    def _():
        ubuf[0:SUBLANES, :] = jnp.zeros((SUBLANES, width), jnp.float32)
        hc[...] = jnp.zeros_like(hc)

    u = uz_ref[0, :, 0:width]
    z = uz_ref[0, :, width:2 * width]
    ubuf[SUBLANES:SUBLANES + rows, :] = u
    xc = cb_ref[...] + cw_ref[CONV_WIDTH - 1:CONV_WIDTH, :] * u
    for k in range(CONV_WIDTH - 1):
        shift = CONV_WIDTH - 1 - k
        xc = xc + cw_ref[k:k + 1, :] * ubuf[SUBLANES - shift:SUBLANES - shift + rows, :]
    ubuf[0:SUBLANES, :] = u[rows - SUBLANES:rows, :]

    gates = _dot(xc.astype(jnp.bfloat16), wg_ref[...]) + bg_ref[...]
    r = _sigmoid(gates[:, 0:width])
    ig = _sigmoid(gates[:, width:2 * width])
    nlam = -lam_ref[...]
    softplus = jnp.maximum(nlam, 0.0) + jnp.log1p(jnp.exp(-jnp.abs(nlam)))
    log_a = (-LRU_C) * r * softplus
    a = jnp.exp(log_a)
    a_s[...] = a
    b_s[...] = jnp.sqrt(1.0 - a * a) * (ig * xc)

    row = lax.broadcasted_iota(jnp.int32, (SUBLANES, width), 0)

    def group(gi, h_prev):
        r0 = pl.multiple_of(gi * SUBLANES, SUBLANES)
        a8 = a_s[pl.ds(r0, SUBLANES), :]
        b8 = b_s[pl.ds(r0, SUBLANES), :]
        d = 1
        while d < SUBLANES:
            a_sh = jnp.where(row >= d, pltpu.roll(a8, d, 0), 1.0)
            b_sh = jnp.where(row >= d, pltpu.roll(b8, d, 0), 0.0)
            b8 = b8 + a8 * b_sh
            a8 = a8 * a_sh
            d *= 2
        h8 = b8 + a8 * h_prev
        b_s[pl.ds(r0, SUBLANES), :] = h8
        return h8[SUBLANES - 1:SUBLANES, :]

    hc[...] = lax.fori_loop(0, rows // SUBLANES, group, hc[...])
    y_ref[0] = (b_s[...] * _silu(z)).astype(y_ref.dtype)


def _lru(uz, conv_w, conv_b, w_gates, b_gates, lam):
    bsz, seq, w2 = uz.shape
    width = w2 // 2
    rows = min(LRU_ROWS, seq)
    assert seq % rows == 0
    const = lambda b, i: (0, 0)
    return pl.pallas_call(
        _lru_kernel,
        out_shape=jax.ShapeDtypeStruct((bsz, seq, width), jnp.bfloat16),
        grid=(bsz, seq // rows),
        in_specs=[pl.BlockSpec((1, rows, w2), lambda b, i: (b, i, 0)),
                  pl.BlockSpec(conv_w.shape, const),
                  pl.BlockSpec((1, width), const),
                  pl.BlockSpec(w_gates.shape, const),
                  pl.BlockSpec((1, w2), const),
                  pl.BlockSpec((1, width), const)],
        out_specs=pl.BlockSpec((1, rows, width), lambda b, i: (b, i, 0)),
        scratch_shapes=[pltpu.VMEM((rows + SUBLANES, width), jnp.float32),
                        pltpu.VMEM((rows, width), jnp.float32),
                        pltpu.VMEM((rows, width), jnp.float32),
                        pltpu.VMEM((1, width), jnp.float32)],
        compiler_params=_params(("parallel", "arbitrary")),
        name="rglru",
    )(uz, conv_w, conv_b.reshape(1, width), w_gates, b_gates.reshape(1, w2), lam.reshape(1, width))


def _cmp_kernel(q_ref, kc_ref, vc_ref, kg_ref, pek_ref, pev_ref, wk_ref, wv_ref, qaug_ref, ocmp_ref):
    seq = q_ref.shape[1]
    n_blk = seq // CMP_STRIDE
    n_cmp = (seq - CMP_LEN) // CMP_STRIDE + 1
    n_slc = seq // SLC_LEN
    half = CMP_LEN // CMP_STRIDE
    assert half == 2 and n_blk == LANES
    kvw = NSA_KV_HEADS * HEAD_DIM
    heads_per_blk = LANES // HEAD_DIM

    acc = [jnp.zeros((n_blk, kvw), jnp.float32) for _ in range(4)]
    for l in range(CMP_STRIDE):
        xk = kc_ref[0, pl.ds(l, n_blk, stride=CMP_STRIDE), :]
        xv = vc_ref[0, pl.ds(l, n_blk, stride=CMP_STRIDE), :]
        l2 = CMP_STRIDE + l
        acc[0] = acc[0] + _dot((xk + pek_ref[l:l + 1, :]).astype(jnp.bfloat16), wk_ref[l])
        acc[1] = acc[1] + _dot((xk + pek_ref[l2:l2 + 1, :]).astype(jnp.bfloat16), wk_ref[l2])
        acc[2] = acc[2] + _dot((xv + pev_ref[l:l + 1, :]).astype(jnp.bfloat16), wv_ref[l])
        acc[3] = acc[3] + _dot((xv + pev_ref[l2:l2 + 1, :]).astype(jnp.bfloat16), wv_ref[l2])
    k_cmp = acc[0] + pltpu.roll(acc[1], n_blk - 1, 0)
    v_cmp = acc[2] + pltpu.roll(acc[3], n_blk - 1, 0)
    v_cmp_t = v_cmp.T.astype(jnp.bfloat16)
    k_n = [_head_rmsnorm(k_cmp[:, g * HEAD_DIM:(g + 1) * HEAD_DIM], kg_ref[...]).astype(jnp.bfloat16)
           for g in range(NSA_KV_HEADS)]
    v_t = [v_cmp_t[g * HEAD_DIM:(g + 1) * HEAD_DIM, :] for g in range(NSA_KV_HEADS)]

    n_idx = lax.broadcasted_iota(jnp.int32, (n_blk, TQ), 0)
    j_idx = lax.broadcasted_iota(jnp.int32, (n_slc, TQ), 0)
    oj = lax.broadcasted_iota(jnp.int32, (n_slc, n_blk), 0) * SLC_LEN
    on = lax.broadcasted_iota(jnp.int32, (n_slc, n_blk), 1) * CMP_STRIDE
    overlap_t = ((on <= oj + SLC_LEN - 1) & (on + CMP_LEN - 1 >= oj)
                 & (on < n_cmp * CMP_STRIDE)).astype(jnp.bfloat16)

    def q_tile(qi, carry):
        t0 = pl.multiple_of(qi * TQ, TQ)
        t_row = t0 + lax.broadcasted_iota(jnp.int32, (1, TQ), 1)
        valid_cmp = (n_idx * CMP_STRIDE + CMP_LEN - 1 <= t_row) & (n_idx < n_cmp)
        blk_t = t_row // SLC_LEN
        valid_slc = j_idx <= blk_t
        forced = (j_idx == 0) | (j_idx == blk_t) | (j_idx == blk_t - 1)
        for g in range(NSA_KV_HEADS):
            p_sum = jnp.zeros((n_blk, TQ), jnp.float32)
            q_heads = []
            for r in range(NSA_GROUP):
                h = g * NSA_GROUP + r
                blk = h // heads_per_blk
                pair = q_ref[0, pl.ds(t0, TQ), blk * LANES:(blk + 1) * LANES].astype(jnp.float32)
                j = h % heads_per_blk
                qh = pair[:, j * HEAD_DIM:(j + 1) * HEAD_DIM].astype(jnp.bfloat16)
                q_heads.append(qh)
                s = _dot_nt(k_n[g], qh)
                s = jnp.where(valid_cmp, s, NEG_INF)
                m = jnp.max(s, axis=0, keepdims=True)
                e = jnp.where(valid_cmp, jnp.exp(s - m), 0.0)
                p = e / jnp.maximum(jnp.sum(e, axis=0, keepdims=True), 1e-30)
                ocmp_ref[0, qi, h * HEAD_DIM:(h + 1) * HEAD_DIM, :] = _dot(v_t[g], p.astype(jnp.bfloat16))
                p_sum = p_sum + p
            imp = _dot01_left(overlap_t, p_sum)
            score = jnp.where(valid_slc, jnp.where(forced, FORCE_SCORE, imp), NEG_INF)
            rank = jnp.zeros((n_slc, TQ), jnp.float32)
            for jp in range(n_slc):
                other = score[jp:jp + 1, :]
                ahead = (other > score) | ((other == score) & (j_idx > jp))
                rank = rank + jnp.where(ahead, 1.0, 0.0)
            bias_t = jnp.where((rank < N_SELECT) & valid_slc, 0.0, NEG_INF)
            bias_t = jnp.concatenate([bias_t, jnp.zeros((LANES - n_slc, TQ), jnp.float32)], axis=0)
            bias = bias_t.T[:, 0:HEAD_DIM].astype(jnp.bfloat16)
            for r in range(NSA_GROUP):
                qaug_ref[0, g * NSA_GROUP + r, pl.ds(t0, TQ), :] = jnp.concatenate([q_heads[r], bias], axis=-1)
        return carry

    lax.fori_loop(0, seq // TQ, q_tile, 0)


def _nsa_cmp(nqn, kc, vc, k_g0, pe_k2, pe_v2, wk_bd, wv_bd):
    bsz, seq, qw = nqn.shape
    n_heads = qw // HEAD_DIM
    nq_t = seq // TQ
    assert seq // SLC_LEN <= HEAD_DIM
    c2 = lambda b: (0, 0)
    c3 = lambda b: (0, 0, 0)
    return pl.pallas_call(
        _cmp_kernel,
        out_shape=[jax.ShapeDtypeStruct((bsz, n_heads, seq, LANES), jnp.bfloat16),
                   jax.ShapeDtypeStruct((bsz, nq_t, qw, TQ), jnp.float32)],
        grid=(bsz,),
        in_specs=[pl.BlockSpec((1, seq, qw), lambda b: (b, 0, 0)),
                  pl.BlockSpec((1, seq, kc.shape[2]), lambda b: (b, 0, 0)),
                  pl.BlockSpec((1, seq, vc.shape[2]), lambda b: (b, 0, 0)),
                  pl.BlockSpec((1, HEAD_DIM), c2),
                  pl.BlockSpec(pe_k2.shape, c2),
                  pl.BlockSpec(pe_v2.shape, c2),
                  pl.BlockSpec(wk_bd.shape, c3),
                  pl.BlockSpec(wv_bd.shape, c3)],
        out_specs=[pl.BlockSpec((1, n_heads, seq, LANES), lambda b: (b, 0, 0, 0)),
                   pl.BlockSpec((1, nq_t, qw, TQ), lambda b: (b, 0, 0, 0))],
        compiler_params=_params(("parallel",)),
        name="nsa_compress",
    )(nqn, kc, vc, k_g0.reshape(1, HEAD_DIM), pe_k2, pe_v2, wk_bd, wv_bd)


def _flash_chunks(qs, ks, v_ts, states, keep=None):
    heads = range(len(qs))
    ss = [_dot_nt(ks[h], qs[h]) for h in heads]
    if keep is not None:
        ss = [jnp.where(keep, s, NEG_INF) for s in ss]
    m_new = [jnp.maximum(states[h][0], jnp.max(ss[h], axis=0, keepdims=True)) for h in heads]
    ps = [jnp.exp(ss[h] - m_new[h]) for h in heads]
    out = []
    for h in heads:
        m, l, acc = states[h]
        alpha = jnp.exp(m - m_new[h])
        l = alpha * l + jnp.sum(ps[h], axis=0, keepdims=True)
        acc = alpha * acc
        for j, v_t in enumerate(v_ts[h]):
            w = v_t.shape[1]
            acc = acc + _dot(v_t, ps[h][j * w:(j + 1) * w, :].astype(jnp.bfloat16))
        out.append((m_new[h], l, acc))
    return tuple(out)


def _flash_init():
    return (jnp.full((1, TQ), NEG_INF, jnp.float32), jnp.zeros((1, TQ), jnp.float32),
            jnp.zeros((HEAD_DIM, TQ), jnp.float32))


def _flash_out(state):
    _, l, acc = state
    return acc / jnp.maximum(l, 1e-30)


def _causal_flash(qs, k_of, vt_of, qi):
    heads = range(len(qs))
    t0 = qi * TQ
    k_pos = lax.broadcasted_iota(jnp.int32, (KC, TQ), 0)
    q_pos = lax.broadcasted_iota(jnp.int32, (KC, TQ), 1)

    def chunk(kc, states, masked):
        k0 = pl.multiple_of(kc * KC, KC)
        keep = ((k0 + k_pos) <= (t0 + q_pos)) if masked else None
        return _flash_chunks(qs, [k_of(h, k0) for h in heads], [vt_of(h, kc) for h in heads], states, keep)

    n_full = qi // (KC // TQ)
    st = lax.fori_loop(0, n_full, lambda kc, s: chunk(kc, s, False), tuple(_flash_init() for _ in heads))
    st = chunk(n_full, st, True)
    return [_flash_out(s) for s in st]


def _store_gated(y_ref, zg_ref, outs, t0):
    for pair in range(len(outs) // 2):
        o2 = jnp.concatenate(outs[2 * pair:2 * pair + 2], axis=0).T
        lanes = slice(pair * LANES, (pair + 1) * LANES)
        zg = zg_ref[0, pl.ds(t0, TQ), lanes].astype(jnp.float32)
        y_ref[0, pl.ds(t0, TQ), lanes] = (o2 * zg).astype(y_ref.dtype)


def _nsa_kernel(qaug_ref, ks_ref, kw_ref, vt_ref, gt_ref, zg_ref, ocmp_ref, y_ref):
    seq = ks_ref.shape[1]
    g_id = pl.program_id(1)
    k_pos = lax.broadcasted_iota(jnp.int32, (TQ, TQ), 0)
    q_pos = lax.broadcasted_iota(jnp.int32, (TQ, TQ), 1)
    g_row0 = pl.multiple_of(g_id * GATE_GROUP_STRIDE, GATE_GROUP_STRIDE)
    heads = range(NSA_GROUP)
    vs_rows = slice(HEAD_DIM, 2 * HEAD_DIM)
    vw_rows = slice(3 * HEAD_DIM, 4 * HEAD_DIM)
    tiles_per_chunk = KC // TQ

    def q_tile(qi, carry):
        t0 = pl.multiple_of(qi * TQ, TQ)
        gates = gt_ref[0, qi, pl.ds(g_row0, GATE_GROUP_STRIDE), :]
        qs = [qaug_ref[0, r, pl.ds(t0, TQ), :] for r in heads]

        o_slc = _causal_flash(
            qs, lambda h, k0: ks_ref[0, pl.ds(k0, KC), :],
            lambda h, kc: [vt_ref[0, kc * tiles_per_chunk + j, vs_rows, :] for j in range(tiles_per_chunk)], qi)

        def win_chunk(kc, states):
            k0 = pl.multiple_of(kc * TQ, TQ)
            diff = (t0 + q_pos) - (k0 + k_pos)
            keep = (diff >= 0) & (diff < WINDOW)
            k = kw_ref[0, pl.ds(k0, TQ), :]
            return _flash_chunks(qs, [k] * NSA_GROUP, [[vt_ref[0, kc, vw_rows, :]]] * NSA_GROUP, states, keep)

        st = lax.fori_loop(jnp.maximum(qi - WINDOW // TQ, 0), qi + 1, win_chunk,
                           tuple(_flash_init() for _ in heads))
        o_win = [_flash_out(s) for s in st]

        outs = []
        for r in heads:
            o_cmp = ocmp_ref[0, qi, r * HEAD_DIM:(r + 1) * HEAD_DIM, :]
            outs.append(gates[3 * r:3 * r + 1, :] * o_cmp + gates[3 * r + 1:3 * r + 2, :] * o_slc[r]
                        + gates[3 * r + 2:3 * r + 3, :] * o_win[r])
        _store_gated(y_ref, zg_ref, outs, t0)
        return carry

    lax.fori_loop(0, seq // TQ, q_tile, 0)


def _nsa_attend(qaug, nks, nkw, nvt, ngt, nzg, ocmp_t):
    bsz, n_heads, seq, _ = qaug.shape
    gw = NSA_GROUP * HEAD_DIM
    nq_t = seq // TQ
    assert seq % KC == 0 and WINDOW % TQ == 0
    return pl.pallas_call(
        _nsa_kernel,
        out_shape=jax.ShapeDtypeStruct((bsz, seq, n_heads * HEAD_DIM), jnp.bfloat16),
        grid=(bsz, NSA_KV_HEADS),
        in_specs=[pl.BlockSpec((1, NSA_GROUP, seq, LANES), lambda b, g: (b, g, 0, 0)),
                  pl.BlockSpec((1, seq, LANES), lambda b, g: (b, 0, g)),
                  pl.BlockSpec((1, seq, LANES), lambda b, g: (b, 0, g)),
                  pl.BlockSpec((1, nq_t, gw, TQ), lambda b, g: (b, 0, g, 0)),
                  pl.BlockSpec((1, nq_t, LANES, TQ), lambda b, g: (b, 0, 0, 0)),
                  pl.BlockSpec((1, seq, gw), lambda b, g: (b, 0, g)),
                  pl.BlockSpec((1, nq_t, gw, TQ), lambda b, g: (b, 0, g, 0))],
        out_specs=pl.BlockSpec((1, seq, gw), lambda b, g: (b, 0, g)),
        compiler_params=_params(("parallel", "arbitrary")),
        name="nsa_attend",
    )(qaug, nks, nkw, nvt, ngt, nzg, ocmp_t)


def _fox_kernel(q_ref, k_ref, ex_ref, vt_ref, zg_ref, y_ref):
    seq = q_ref.shape[1]
    n_blk = q_ref.shape[2] // LANES
    heads_per_blk = LANES // HEAD_DIM
    tiles_per_chunk = KC // TQ
    lane = lax.broadcasted_iota(jnp.int32, (TQ, LANES), 1)
    own = [(lane >= j * HEAD_DIM) & (lane < (j + 1) * HEAD_DIM) for j in range(heads_per_blk)]
    ones = [jnp.where((lane >= j * BIAS_TERMS) & (lane < (j + 1) * BIAS_TERMS), 1.0, 0.0).astype(jnp.bfloat16)
            for j in range(heads_per_blk)]

    def q_tile(qi, carry):
        t0 = pl.multiple_of(qi * TQ, TQ)
        qs = []
        for blk in range(n_blk):
            pair = q_ref[0, pl.ds(t0, TQ), blk * LANES:(blk + 1) * LANES]
            for j in range(heads_per_blk):
                qs.append(jnp.concatenate([jnp.where(own[j], pair, jnp.zeros_like(pair)), ones[j]], axis=-1))

        def k_of(h, k0):
            lanes = slice((h // heads_per_blk) * LANES, (h // heads_per_blk + 1) * LANES)
            return jnp.concatenate([k_ref[0, pl.ds(k0, KC), lanes], ex_ref[0, pl.ds(k0, KC), lanes]], axis=-1)

        def vt_of(h, kc):
            return [vt_ref[0, kc * tiles_per_chunk + j, h * HEAD_DIM:(h + 1) * HEAD_DIM, :]
                    for j in range(tiles_per_chunk)]

        _store_gated(y_ref, zg_ref, _causal_flash(qs, k_of, vt_of, qi), t0)
        return carry

    lax.fori_loop(0, seq // TQ, q_tile, 0)


def _fox_attend(fqn, fkn, fex, fvt, fzg):
    bsz, seq, width = fqn.shape
    bw = FOX_HEADS_PER_STEP * HEAD_DIM
    assert seq % KC == 0 and width % bw == 0
    blk = pl.BlockSpec((1, seq, bw), lambda b, p: (b, 0, p))
    return pl.pallas_call(
        _fox_kernel,
        out_shape=jax.ShapeDtypeStruct((bsz, seq, width), jnp.bfloat16),
        grid=(bsz, width // bw),
        in_specs=[blk, blk, blk,
                  pl.BlockSpec((1, seq // TQ, bw, TQ), lambda b, p: (b, 0, p, 0)),
                  blk],
        out_specs=blk,
        compiler_params=_params(("parallel", "arbitrary")),
        name="fox_attend",
    )(fqn, fkn, fex, fvt, fzg)


def _block_diag(blocks):
    n, r, c = blocks.shape
    eye = jnp.eye(n, dtype=blocks.dtype)
    return (eye[:, None, :, None] * blocks[:, :, None, :]).reshape(n * r, n * c)


def _layer_layout(d_model):
    lru_w = d_model // 2
    n_heads = d_model // 128
    aw = n_heads * HEAD_DIM
    kvw = NSA_KV_HEADS * HEAD_DIM
    splits = (lru_w, lru_w, aw, kvw, kvw, kvw, kvw, kvw, kvw, 3 * n_heads, aw, aw, aw, aw, n_heads, aw)
    offs = np.concatenate([[0], np.cumsum(splits)])
    names = ("lru_u", "lru_z", "nq", "kc", "vc", "ks", "vs", "kw", "vw", "gl", "nz", "fq", "fk", "fv", "fl", "fz")
    col = {n: np.arange(offs[i], offs[i + 1]) for i, n in enumerate(names)}
    nkv = []
    for g in range(NSA_KV_HEADS):
        for n in ("ks", "vs", "kw", "vw"):
            nkv.append(col[n][g * HEAD_DIM:(g + 1) * HEAD_DIM])
    misc = np.full((LANES,), -1, np.int64)
    per_group = 3 * NSA_GROUP
    for g in range(NSA_KV_HEADS):
        misc[g * GATE_GROUP_STRIDE:g * GATE_GROUP_STRIDE + per_group] = col["gl"][g * per_group:(g + 1) * per_group]
    misc[FORGET_LANE0:FORGET_LANE0 + n_heads] = col["fl"]
    segs = [("lru", np.concatenate([col["lru_u"], col["lru_z"]])),
            ("nq", col["nq"]),
            ("kc", col["kc"]), ("vc", col["vc"]),
            ("nkv", np.concatenate(nkv)),
            ("nz", col["nz"]),
            ("fq", col["fq"]), ("fk", col["fk"]), ("fv", col["fv"]), ("fz", col["fz"]),
            ("misc", misc)]
    return segs, n_heads


def _permute_w_in(w_in, segs):
    d = w_in.shape[0]
    w_ext = jnp.concatenate([w_in, jnp.zeros((d, 1), w_in.dtype)], axis=1)
    idx = np.concatenate([np.where(c < 0, w_in.shape[1], c) for _, c in segs])
    return w_ext[:, idx].astype(jnp.bfloat16)


def _hybrid_layer(x, norm_g, w_in, w_out, conv_w, conv_b, lru_wa, lru_ba, lru_wx, lru_bx, lru_lambda,
                  nsa_q_g, nsa_k_g, cmp_pe_k, cmp_pe_v, cmp_wk, cmp_wv, nsa_gate_b,
                  fox_q_g, fox_k_g, fox_f_b):
    bsz, seq, d_model = x.shape
    segs, n_heads = _layer_layout(d_model)
    aw = n_heads * HEAD_DIM
    heads_per_blk = LANES // HEAD_DIM
    assert seq % TQ == 0 and seq // CMP_STRIDE == LANES and n_heads * 3 <= 2 * GATE_GROUP_STRIDE
    assert NSA_KV_HEADS * 4 * HEAD_DIM == aw and BIAS_TERMS * heads_per_blk <= LANES
    seg, off = {}, 0
    for name, c in segs:
        seg[name] = (off, off + len(c))
        off += len(c)
    x2d = x.reshape(bsz * seq, d_model)

    ones_bd = _block_diag(jnp.ones((n_heads, HEAD_DIM, HEAD_DIM), jnp.bfloat16))
    one = jnp.ones((HEAD_DIM,), jnp.float32)
    nkv_gain = jnp.concatenate([nsa_k_g[1], one, nsa_k_g[2], one] * NSA_KV_HEADS)
    gain_rows = jnp.stack([jnp.tile(nsa_q_g, n_heads) * ATTN_SCALE, nkv_gain,
                           jnp.tile(fox_q_g, n_heads) * ATTN_SCALE, jnp.tile(fox_k_g, n_heads)])
    per_group = 3 * NSA_GROUP
    misc_bias = jnp.zeros((1, LANES), jnp.float32)
    for g in range(NSA_KV_HEADS):
        misc_bias = misc_bias.at[0, g * GATE_GROUP_STRIDE:g * GATE_GROUP_STRIDE + per_group].set(
            nsa_gate_b[g * per_group:(g + 1) * per_group])
    misc_bias = misc_bias.at[0, FORGET_LANE0:FORGET_LANE0 + n_heads].set(fox_f_b)
    place = np.zeros((BIAS_TERMS, LANES, aw), np.float32)
    for h in range(n_heads):
        for t in range(BIAS_TERMS):
            place[t, FORGET_LANE0 + h, (h // heads_per_blk) * LANES + (h % heads_per_blk) * BIAS_TERMS + t] = 1.0

    (lru, nqn, kc, vc, nks, nkw, nvt, ngt, nzg, fqn, fkn, fex, fvt, fzg) = _in_proj(
        x2d, norm_g, _permute_w_in(w_in, segs), seg, seq, ones_bd, gain_rows, misc_bias,
        jnp.asarray(place, jnp.bfloat16))
    tok = lambda a: a.reshape(bsz, seq, a.shape[1])
    tiled = lambda a: a.reshape(bsz, seq // TQ, a.shape[1], TQ)

    w_gates = jnp.concatenate([_block_diag(lru_wa), _block_diag(lru_wx)], axis=1).astype(jnp.bfloat16)
    y_lru = _lru(tok(lru), conv_w, conv_b, w_gates, jnp.concatenate([lru_ba, lru_bx]), lru_lambda)

    tile2 = lambda a: jnp.tile(a, (1, NSA_KV_HEADS))
    bd2 = lambda w: jax.vmap(lambda m: _block_diag(jnp.stack([m] * NSA_KV_HEADS)))(w).astype(jnp.bfloat16)
    qaug, ocmp_t = _nsa_cmp(tok(nqn), tok(kc), tok(vc), nsa_k_g[0], tile2(cmp_pe_k), tile2(cmp_pe_v),
                            bd2(cmp_wk), bd2(cmp_wv))
    y_nsa = _nsa_attend(qaug, tok(nks), tok(nkw), tiled(nvt), tiled(ngt), tok(nzg), ocmp_t)

    y_fox = _fox_attend(tok(fqn), tok(fkn), tok(fex), tiled(fvt), tok(fzg))

    flat = lambda a: a.reshape(bsz * seq, a.shape[2])
    out = _out_proj(x2d, flat(y_lru), flat(y_nsa), flat(y_fox), w_out.astype(jnp.bfloat16))
    return out.reshape(bsz, seq, d_model)


def kernel(x, norm_g, w_in, w_out, conv_w, conv_b, lru_wa, lru_ba, lru_wx, lru_bx, lru_lambda, nsa_q_g, nsa_k_g, cmp_pe_k, cmp_pe_v, cmp_wk, cmp_wv, nsa_gate_b, fox_q_g, fox_k_g, fox_f_b):
    for l in range(norm_g.shape[0]):
        x = _hybrid_layer(x, norm_g[l], w_in[l], w_out[l], conv_w[l], conv_b[l], lru_wa[l], lru_ba[l],
                          lru_wx[l], lru_bx[l], lru_lambda[l], nsa_q_g[l], nsa_k_g[l], cmp_pe_k[l],
                          cmp_pe_v[l], cmp_wk[l], cmp_wv[l], nsa_gate_b[l], fox_q_g[l], fox_k_g[l],
                          fox_f_b[l])
    return x
```

```python
import functools

import jax
import jax.numpy as jnp
import numpy as np
from jax import lax
from jax.experimental import pallas as pl
from jax.experimental.pallas import tpu as pltpu

HEAD_DIM = 64
LRU_BLOCKS = 8
CONV_WIDTH = 4
LRU_C = 8.0
NSA_KV_HEADS = 2
NSA_GROUP = 4
CMP_LEN = 32
CMP_STRIDE = 16
SLC_LEN = 64
N_SELECT = 16
WINDOW = 512
NORM_EPS = 1e-6
NEG_INF = -1e30
FORCE_SCORE = 1e9
ATTN_SCALE = HEAD_DIM ** -0.5
LOG2E = 1.4426950408889634

LANES = 128
SUBLANES = 8
TQ = 256
KC = 512
PROJ_ROWS = TQ
LRU_ROWS = 512
VMEM_LIMIT = 56 * 1024 * 1024

GATE_GROUP_STRIDE = 16
FORGET_LANE0 = 32
FOX_HEADS_PER_STEP = 4
BIAS_TERMS = 3
VROWS = 80

_NT = (((1,), (1,)), ((), ()))


def _dot(a, b):
    return jnp.dot(a, b, preferred_element_type=jnp.float32)


def _dot_nt(a, b):
    return lax.dot_general(a, b, _NT, preferred_element_type=jnp.float32)


def _split3(x):
    hi = x.astype(jnp.bfloat16)
    r = x - hi.astype(jnp.float32)
    mid = r.astype(jnp.bfloat16)
    lo = (r - mid.astype(jnp.float32)).astype(jnp.bfloat16)
    return hi, mid, lo


def _dot01_left(m01, x):
    hi, mid, lo = _split3(x)
    return _dot(m01, hi) + _dot(m01, mid) + _dot(m01, lo)


def _head_rmsnorm(x, gain):
    ms = jnp.sum(x * x, axis=-1, keepdims=True) * (1.0 / HEAD_DIM)
    return x * lax.rsqrt(ms + NORM_EPS) * gain


def _heads_rmsnorm(x, ones_bd, gain_row):
    sq = x * x
    hi = sq.astype(jnp.bfloat16)
    lo = (sq - hi.astype(jnp.float32)).astype(jnp.bfloat16)
    ss = _dot(hi, ones_bd) + _dot(lo, ones_bd)
    return x * lax.rsqrt(ss * (1.0 / HEAD_DIM) + NORM_EPS) * gain_row


def _sigmoid(x):
    return 1.0 / (1.0 + jnp.exp(-x))


def _silu(x):
    return x * _sigmoid(x)


def _params(sem):
    return pltpu.CompilerParams(dimension_semantics=sem, vmem_limit_bytes=VMEM_LIMIT)


def _in_proj_kernel(seg, tiles_per_seq, x_ref, g_ref, w_ref, bd_ref, gain_ref, mb_ref, place_ref,
                    lru_ref, nqn_ref, kc_ref, vc_ref, nks_ref, nkw_ref, nvt_ref, ngt_ref, nzg_ref,
                    fqn_ref, fkn_ref, fex_ref, fvt_ref, fzg_ref, carry):
    i = pl.program_id(0)
    rows = x_ref.shape[0]
    x = x_ref[...]
    ms = jnp.mean(x * x, axis=-1, keepdims=True)
    h = (x * lax.rsqrt(ms + NORM_EPS) * g_ref[...]).astype(jnp.bfloat16)
    proj = lambda name: _dot(h, w_ref[:, seg[name][0]:seg[name][1]])
    bd = bd_ref[...]
    bf = jnp.bfloat16

    lru_ref[...] = proj("lru")
    kc_ref[...] = proj("kc")
    vc_ref[...] = proj("vc")
    nqn_ref[...] = _heads_rmsnorm(proj("nq"), bd, gain_ref[0:1, :]).astype(bf)
    nzg_ref[...] = _silu(proj("nz")).astype(bf)
    fqn_ref[...] = _heads_rmsnorm(proj("fq"), bd, gain_ref[2:3, :]).astype(bf)
    fkn_ref[...] = _heads_rmsnorm(proj("fk"), bd, gain_ref[3:4, :]).astype(bf)
    fzg_ref[...] = _silu(proj("fz")).astype(bf)
    pad_row = lax.broadcasted_iota(jnp.int32, (VROWS - HEAD_DIM, rows), 0)
    ones_pad = jnp.where(pad_row == 0, 1.0, 0.0).astype(bf)

    def store_vt(ref, slot, v_t):
        ref[0, slot * VROWS:slot * VROWS + HEAD_DIM, :] = v_t.astype(bf)
        ref[0, slot * VROWS + HEAD_DIM:(slot + 1) * VROWS, :] = ones_pad

    fv = proj("fv")
    for j in range(fv.shape[1] // LANES):
        fv_t = fv[:, j * LANES:(j + 1) * LANES].T
        store_vt(fvt_ref, 2 * j, fv_t[0:HEAD_DIM, :])
        store_vt(fvt_ref, 2 * j + 1, fv_t[HEAD_DIM:2 * HEAD_DIM, :])

    nkv = proj("nkv")
    nkv_n = _heads_rmsnorm(nkv, bd, gain_ref[1:2, :])
    lane = lax.broadcasted_iota(jnp.int32, (rows, LANES), 1)
    pos = (i % tiles_per_seq) * rows + lax.broadcasted_iota(jnp.int32, (rows, LANES), 0)
    onehot = jnp.where(lane - HEAD_DIM == pos // SLC_LEN, 1.0, 0.0)
    for g in range(NSA_KV_HEADS):
        b0 = 2 * g * LANES
        nks_ref[:, g * LANES:(g + 1) * LANES] = jnp.where(lane < HEAD_DIM, nkv_n[:, b0:b0 + LANES], onehot).astype(bf)
        nkw_ref[:, g * LANES:(g + 1) * LANES] = jnp.where(lane < HEAD_DIM, nkv_n[:, b0 + LANES:b0 + 2 * LANES],
                                                          0.0).astype(bf)
    for j in range(nkv.shape[1] // LANES):
        store_vt(nvt_ref, j, nkv[:, j * LANES:(j + 1) * LANES].T[HEAD_DIM:2 * HEAD_DIM, :])

    misc = proj("misc") + mb_ref[...]
    ngt_ref[0] = _sigmoid(misc).T
    log_f = jnp.minimum(misc, 0.0) - jnp.log1p(jnp.exp(-jnp.abs(misc)))

    @pl.when(i % tiles_per_seq == 0)
    def _():
        carry[...] = jnp.zeros_like(carry)

    tri = (lax.broadcasted_iota(jnp.int32, (rows, rows), 0)
           >= lax.broadcasted_iota(jnp.int32, (rows, rows), 1)).astype(bf)
    cum = _dot01_left(tri, log_f) + carry[...]
    carry[...] = cum[rows - 1:rows, :]
    terms = _split3(cum * (-LOG2E))
    fex = _dot(terms[0], place_ref[0])
    for t in range(1, BIAS_TERMS):
        fex = fex + _dot(terms[t], place_ref[t])
    fex_ref[...] = fex.astype(bf)


def _in_proj(x2d, norm_g, w_perm, seg, seq, ones_bd, gain_rows, misc_bias, place):
    n, d = x2d.shape
    aw = seg["nq"][1] - seg["nq"][0]
    rows = PROJ_ROWS
    assert n % rows == 0 and seq % rows == 0
    n_tiles = n // rows
    f32, bf = jnp.float32, jnp.bfloat16
    row_out = lambda w, dt: (jax.ShapeDtypeStruct((n, w), dt), pl.BlockSpec((rows, w), lambda i: (i, 0)))
    t_out = lambda w, dt: (jax.ShapeDtypeStruct((n_tiles, w, rows), dt), pl.BlockSpec((1, w, rows), lambda i: (i, 0, 0)))
    outs = [row_out(seg["lru"][1] - seg["lru"][0], f32),
            row_out(aw, bf),
            row_out(LANES, f32), row_out(LANES, f32),
            row_out(NSA_KV_HEADS * LANES, bf),
            row_out(NSA_KV_HEADS * LANES, bf),
            t_out(2 * NSA_KV_HEADS * VROWS, bf),
            t_out(LANES, f32),
            row_out(aw, bf),
            row_out(aw, bf), row_out(aw, bf), row_out(aw, bf),
            t_out(aw // HEAD_DIM * VROWS, bf),
            row_out(aw, bf)]
    const2 = lambda a: pl.BlockSpec(a.shape, lambda i: (0, 0))
    return pl.pallas_call(
        functools.partial(_in_proj_kernel, seg, seq // rows),
        out_shape=[o[0] for o in outs],
        grid=(n_tiles,),
        in_specs=[pl.BlockSpec((rows, d), lambda i: (i, 0)),
                  pl.BlockSpec((1, d), lambda i: (0, 0)),
                  const2(w_perm), const2(ones_bd), const2(gain_rows), const2(misc_bias),
                  pl.BlockSpec(place.shape, lambda i: (0, 0, 0))],
        out_specs=[o[1] for o in outs],
        scratch_shapes=[pltpu.VMEM((1, LANES), jnp.float32)],
        compiler_params=_params(("arbitrary",)),
        name="in_proj",
    )(x2d, norm_g.reshape(1, d), w_perm, ones_bd, gain_rows, misc_bias, place)


def _out_proj_kernel(x_ref, ya_ref, yb_ref, yc_ref, w_ref, o_ref):
    wa = ya_ref.shape[1]
    wb = yb_ref.shape[1]
    acc = x_ref[...]
    acc = acc + _dot(ya_ref[...], w_ref[0:wa, :])
    acc = acc + _dot(yb_ref[...], w_ref[wa:wa + wb, :])
    acc = acc + _dot(yc_ref[...], w_ref[wa + wb:, :])
    o_ref[...] = acc


def _out_proj(x2d, ya, yb, yc, w_out_bf16):
    n, d = x2d.shape
    row = lambda i: (i, 0)
    return pl.pallas_call(
        _out_proj_kernel,
        out_shape=jax.ShapeDtypeStruct((n, d), jnp.float32),
        grid=(n // PROJ_ROWS,),
        in_specs=[pl.BlockSpec((PROJ_ROWS, d), row),
                  pl.BlockSpec((PROJ_ROWS, ya.shape[1]), row),
                  pl.BlockSpec((PROJ_ROWS, yb.shape[1]), row),
                  pl.BlockSpec((PROJ_ROWS, yc.shape[1]), row),
                  pl.BlockSpec(w_out_bf16.shape, lambda i: (0, 0))],
        out_specs=pl.BlockSpec((PROJ_ROWS, d), row),
        compiler_params=_params(("parallel",)),
        name="out_proj",
    )(x2d, ya, yb, yc, w_out_bf16)


def _lru_kernel(uz_ref, cw_ref, cb_ref, wg_ref, bg_ref, lam_ref, y_ref, ubuf, a_s, b_s, hc):
    rows = a_s.shape[0]
    width = a_s.shape[1]

    @pl.when(pl.program_id(1) == 0)
    def _():
        ubuf[0:SUBLANES, :] = jnp.zeros((SUBLANES, width), jnp.float32)
        hc[...] = jnp.zeros_like(hc)

    u = uz_ref[0, :, 0:width]
    z = uz_ref[0, :, width:2 * width]
    ubuf[SUBLANES:SUBLANES + rows, :] = u
    xc = cb_ref[...] + cw_ref[CONV_WIDTH - 1:CONV_WIDTH, :] * u
    for k in range(CONV_WIDTH - 1):
        shift = CONV_WIDTH - 1 - k
        xc = xc + cw_ref[k:k + 1, :] * ubuf[SUBLANES - shift:SUBLANES - shift + rows, :]
    ubuf[0:SUBLANES, :] = u[rows - SUBLANES:rows, :]

    gates = _dot(xc.astype(jnp.bfloat16), wg_ref[...]) + bg_ref[...]
    r = _sigmoid(gates[:, 0:width])
    ig = _sigmoid(gates[:, width:2 * width])
    nlam = -lam_ref[...]
    softplus = jnp.maximum(nlam, 0.0) + jnp.log1p(jnp.exp(-jnp.abs(nlam)))
    log_a = (-LRU_C) * r * softplus
    a = jnp.exp(log_a)
    a_s[...] = a
    b_s[...] = jnp.sqrt(1.0 - a * a) * (ig * xc)

    row = lax.broadcasted_iota(jnp.int32, (SUBLANES, width), 0)

    def group(gi, h_prev):
        r0 = pl.multiple_of(gi * SUBLANES, SUBLANES)
        a8 = a_s[pl.ds(r0, SUBLANES), :]
        b8 = b_s[pl.ds(r0, SUBLANES), :]
        d = 1
        while d < SUBLANES:
            a_sh = jnp.where(row >= d, pltpu.roll(a8, d, 0), 1.0)
            b_sh = jnp.where(row >= d, pltpu.roll(b8, d, 0), 0.0)
            b8 = b8 + a8 * b_sh
            a8 = a8 * a_sh
            d *= 2
        h8 = b8 + a8 * h_prev
        b_s[pl.ds(r0, SUBLANES), :] = h8
        return h8[SUBLANES - 1:SUBLANES, :]

    hc[...] = lax.fori_loop(0, rows // SUBLANES, group, hc[...])
    y_ref[0] = (b_s[...] * _silu(z)).astype(y_ref.dtype)


def _lru(uz, conv_w, conv_b, w_gates, b_gates, lam):
    bsz, seq, w2 = uz.shape
    width = w2 // 2
    rows = min(LRU_ROWS, seq)
    assert seq % rows == 0
    const = lambda b, i: (0, 0)
    return pl.pallas_call(
        _lru_kernel,
        out_shape=jax.ShapeDtypeStruct((bsz, seq, width), jnp.bfloat16),
        grid=(bsz, seq // rows),
        in_specs=[pl.BlockSpec((1, rows, w2), lambda b, i: (b, i, 0)),
                  pl.BlockSpec(conv_w.shape, const),
                  pl.BlockSpec((1, width), const),
                  pl.BlockSpec(w_gates.shape, const),
                  pl.BlockSpec((1, w2), const),
                  pl.BlockSpec((1, width), const)],
        out_specs=pl.BlockSpec((1, rows, width), lambda b, i: (b, i, 0)),
        scratch_shapes=[pltpu.VMEM((rows + SUBLANES, width), jnp.float32),
                        pltpu.VMEM((rows, width), jnp.float32),
                        pltpu.VMEM((rows, width), jnp.float32),
                        pltpu.VMEM((1, width), jnp.float32)],
        compiler_params=_params(("parallel", "arbitrary")),
        name="rglru",
    )(uz, conv_w, conv_b.reshape(1, width), w_gates, b_gates.reshape(1, w2), lam.reshape(1, width))


def _cmp_kernel(q_ref, kc_ref, vc_ref, kg_ref, pek_ref, pev_ref, wk_ref, wv_ref, qaug_ref, ocmp_ref):
    seq = q_ref.shape[1]
    n_blk = seq // CMP_STRIDE
    n_cmp = (seq - CMP_LEN) // CMP_STRIDE + 1
    n_slc = seq // SLC_LEN
    half = CMP_LEN // CMP_STRIDE
    assert half == 2 and n_blk == LANES
    kvw = NSA_KV_HEADS * HEAD_DIM
    heads_per_blk = LANES // HEAD_DIM

    acc = [jnp.zeros((n_blk, kvw), jnp.float32) for _ in range(4)]
    for l in range(CMP_STRIDE):
        xk = kc_ref[0, pl.ds(l, n_blk, stride=CMP_STRIDE), :]
        xv = vc_ref[0, pl.ds(l, n_blk, stride=CMP_STRIDE), :]
        l2 = CMP_STRIDE + l
        acc[0] = acc[0] + _dot((xk + pek_ref[l:l + 1, :]).astype(jnp.bfloat16), wk_ref[l])
        acc[1] = acc[1] + _dot((xk + pek_ref[l2:l2 + 1, :]).astype(jnp.bfloat16), wk_ref[l2])
        acc[2] = acc[2] + _dot((xv + pev_ref[l:l + 1, :]).astype(jnp.bfloat16), wv_ref[l])
        acc[3] = acc[3] + _dot((xv + pev_ref[l2:l2 + 1, :]).astype(jnp.bfloat16), wv_ref[l2])
    k_cmp = acc[0] + pltpu.roll(acc[1], n_blk - 1, 0)
    v_cmp = acc[2] + pltpu.roll(acc[3], n_blk - 1, 0)
    v_cmp_t = v_cmp.T.astype(jnp.bfloat16)
    k_n = [_head_rmsnorm(k_cmp[:, g * HEAD_DIM:(g + 1) * HEAD_DIM], kg_ref[...]).astype(jnp.bfloat16)
           for g in range(NSA_KV_HEADS)]
    v_t = [v_cmp_t[g * HEAD_DIM:(g + 1) * HEAD_DIM, :] for g in range(NSA_KV_HEADS)]

    n_idx = lax.broadcasted_iota(jnp.int32, (n_blk, TQ), 0)
    j_idx = lax.broadcasted_iota(jnp.int32, (n_slc, TQ), 0)
    oj = lax.broadcasted_iota(jnp.int32, (n_slc, n_blk), 0) * SLC_LEN
    on = lax.broadcasted_iota(jnp.int32, (n_slc, n_blk), 1) * CMP_STRIDE
    overlap_t = ((on <= oj + SLC_LEN - 1) & (on + CMP_LEN - 1 >= oj)
                 & (on < n_cmp * CMP_STRIDE)).astype(jnp.bfloat16)

    def q_tile(qi, carry):
        t0 = pl.multiple_of(qi * TQ, TQ)
        t_row = t0 + lax.broadcasted_iota(jnp.int32, (1, TQ), 1)
        valid_cmp = (n_idx * CMP_STRIDE + CMP_LEN - 1 <= t_row) & (n_idx < n_cmp)
        blk_t = t_row // SLC_LEN
        valid_slc = j_idx <= blk_t
        forced = (j_idx == 0) | (j_idx == blk_t) | (j_idx == blk_t - 1)
        for g in range(NSA_KV_HEADS):
            p_sum = jnp.zeros((n_blk, TQ), jnp.float32)
            q_heads = []
            for r in range(NSA_GROUP):
                h = g * NSA_GROUP + r
                blk = h // heads_per_blk
                pair = q_ref[0, pl.ds(t0, TQ), blk * LANES:(blk + 1) * LANES].astype(jnp.float32)
                j = h % heads_per_blk
                qh = pair[:, j * HEAD_DIM:(j + 1) * HEAD_DIM].astype(jnp.bfloat16)
                q_heads.append(qh)
                s = _dot_nt(k_n[g], qh)
                s = jnp.where(valid_cmp, s, NEG_INF)
                m = jnp.max(s, axis=0, keepdims=True)
                e = jnp.where(valid_cmp, jnp.exp2(s - m), 0.0)
                p = e / jnp.maximum(jnp.sum(e, axis=0, keepdims=True), 1e-30)
                ocmp_ref[0, qi, h * HEAD_DIM:(h + 1) * HEAD_DIM, :] = _dot(v_t[g], p.astype(jnp.bfloat16))
                p_sum = p_sum + p
            imp = _dot01_left(overlap_t, p_sum)
            score = jnp.where(valid_slc, jnp.where(forced, FORCE_SCORE, imp), NEG_INF)
            rank = jnp.zeros((n_slc, TQ), jnp.float32)
            for jp in range(n_slc):
                other = score[jp:jp + 1, :]
                ahead = (other > score) | ((other == score) & (j_idx > jp))
                rank = rank + jnp.where(ahead, 1.0, 0.0)
            bias_t = jnp.where((rank < N_SELECT) & valid_slc, 0.0, NEG_INF)
            bias_t = jnp.concatenate([bias_t, jnp.zeros((LANES - n_slc, TQ), jnp.float32)], axis=0)
            bias = bias_t.T[:, 0:HEAD_DIM].astype(jnp.bfloat16)
            for r in range(NSA_GROUP):
                qaug_ref[0, g * NSA_GROUP + r, pl.ds(t0, TQ), :] = jnp.concatenate([q_heads[r], bias], axis=-1)
        return carry

    lax.fori_loop(0, seq // TQ, q_tile, 0)


def _nsa_cmp(nqn, kc, vc, k_g0, pe_k2, pe_v2, wk_bd, wv_bd):
    bsz, seq, qw = nqn.shape
    n_heads = qw // HEAD_DIM
    nq_t = seq // TQ
    assert seq // SLC_LEN <= HEAD_DIM
    c2 = lambda b: (0, 0)
    c3 = lambda b: (0, 0, 0)
    return pl.pallas_call(
        _cmp_kernel,
        out_shape=[jax.ShapeDtypeStruct((bsz, n_heads, seq, LANES), jnp.bfloat16),
                   jax.ShapeDtypeStruct((bsz, nq_t, qw, TQ), jnp.float32)],
        grid=(bsz,),
        in_specs=[pl.BlockSpec((1, seq, qw), lambda b: (b, 0, 0)),
                  pl.BlockSpec((1, seq, kc.shape[2]), lambda b: (b, 0, 0)),
                  pl.BlockSpec((1, seq, vc.shape[2]), lambda b: (b, 0, 0)),
                  pl.BlockSpec((1, HEAD_DIM), c2),
                  pl.BlockSpec(pe_k2.shape, c2),
                  pl.BlockSpec(pe_v2.shape, c2),
                  pl.BlockSpec(wk_bd.shape, c3),
                  pl.BlockSpec(wv_bd.shape, c3)],
        out_specs=[pl.BlockSpec((1, n_heads, seq, LANES), lambda b: (b, 0, 0, 0)),
                   pl.BlockSpec((1, nq_t, qw, TQ), lambda b: (b, 0, 0, 0))],
        compiler_params=_params(("parallel",)),
        name="nsa_compress",
    )(nqn, kc, vc, k_g0.reshape(1, HEAD_DIM), pe_k2, pe_v2, wk_bd, wv_bd)


ITEM_FIELDS = 5
MASK_NONE, MASK_A, MASK_B, MASK_ALL = 0, 1, 2, 3


def _item_table(n_q_tiles, keys_per_chunk, window_chunks=None):
    items = []
    for qi in range(n_q_tiles):
        if window_chunks is None:
            n_full = qi * TQ // keys_per_chunk
            chunks = list(range(n_full + 1))
            masks = [MASK_NONE] * n_full + [MASK_A if qi * TQ % keys_per_chunk == 0 else MASK_B]
        else:
            chunks = list(range(max(qi - window_chunks, 0), qi + 1))
            masks = [MASK_B if c == qi - window_chunks else MASK_NONE for c in chunks]
            masks[-1] = MASK_A
        for n, (c, mk) in enumerate(zip(chunks, masks)):
            items.append((qi, c, int(n == 0), int(n == len(chunks) - 1), mk))
    if len(items) % 2:
        items.append((items[-1][0], items[-1][1], 0, 0, MASK_ALL))
    return np.asarray(items, np.int32).T


def _fill_masks(mask_s, keep_a, keep_b):
    n = mask_s.shape[1]
    r = lax.broadcasted_iota(jnp.int32, (n, TQ), 0)
    q = lax.broadcasted_iota(jnp.int32, (n, TQ), 1)
    mask_s[MASK_NONE] = jnp.zeros((n, TQ), jnp.float32)
    mask_s[MASK_A] = jnp.where(keep_a(r, q), 0.0, NEG_INF)
    mask_s[MASK_B] = jnp.where(keep_b(r, q), 0.0, NEG_INF)
    mask_s[MASK_ALL] = jnp.full((n, TQ), NEG_INF, jnp.float32)


def _flash_pipeline(tbl_ref, row0, n_items, n_keys, n_heads, qs_of, k_of, vt_of, s_buf, mask_s, on_last):
    heads = range(n_heads)

    def produce(t, slot, h, qs, bias):
        s = _dot_nt(k_of(h, tbl_ref[row0 + 1, t]), qs[h]) + bias
        s_buf[slot, h, 0:n_keys, :] = s
        return jnp.max(s, axis=0, keepdims=True)

    def item(t, slot, carry):
        states, col_max = carry
        qi = tbl_ref[row0, t]
        chunk = tbl_ref[row0 + 1, t]
        first = tbl_ref[row0 + 2, t] == 1
        last = tbl_ref[row0 + 3, t]
        t_next = jnp.minimum(t + 1, n_items - 1)
        qs_next = qs_of(tbl_ref[row0, t_next])
        bias_next = mask_s[tbl_ref[row0 + 4, t_next]]
        out, col_max_next = [], []
        for h in heads:
            col_max_next.append(produce(t_next, 1 - slot, h, qs_next, bias_next))
            m_prev = jnp.where(first, NEG_INF, states[h][0])
            m_new = jnp.maximum(m_prev, col_max[h])
            p = jnp.exp2(s_buf[slot, h, 0:n_keys, :] - m_new).astype(jnp.bfloat16)
            acc = jnp.exp2(m_prev - m_new) * jnp.where(first, 0.0, states[h][1])
            for j, v_t in enumerate(vt_of(h, chunk)):
                acc = acc + _dot(v_t, p[j * TQ:(j + 1) * TQ, :])
            out.append((m_new, acc))

        @pl.when(last == 1)
        def _():
            on_last(qi, [o[1] for o in out])

        return tuple(out), tuple(col_max_next)

    qs0 = qs_of(tbl_ref[row0, 0])
    bias0 = mask_s[tbl_ref[row0 + 4, 0]]
    col_max0 = tuple(produce(0, 0, h, qs0, bias0) for h in heads)
    init = tuple((jnp.full((1, TQ), NEG_INF, jnp.float32), jnp.zeros((VROWS, TQ), jnp.float32)) for _ in heads)
    lax.fori_loop(0, n_items // 2, lambda i, c: item(2 * i + 1, 1, item(2 * i, 0, c)), (init, col_max0))


def _normalise(acc):
    return acc[0:HEAD_DIM, :] / jnp.maximum(acc[HEAD_DIM:HEAD_DIM + 1, :], 1e-30)


def _store_gated(y_ref, zg_ref, outs, t0):
    for pair in range(len(outs) // 2):
        o2 = jnp.concatenate(outs[2 * pair:2 * pair + 2], axis=0).T
        lanes = slice(pair * LANES, (pair + 1) * LANES)
        zg = zg_ref[0, pl.ds(t0, TQ), lanes].astype(jnp.float32)
        y_ref[0, pl.ds(t0, TQ), lanes] = (o2 * zg).astype(y_ref.dtype)


def _nsa_kernel(n_slc_items, n_win_items, tbl_ref, qaug_ref, ks_ref, kw_ref, vt_ref, gt_ref, zg_ref, ocmp_ref,
                y_ref, s_buf, mask_c, mask_w, oslc_s):
    g_id = pl.program_id(1)
    g_row0 = pl.multiple_of(g_id * GATE_GROUP_STRIDE, GATE_GROUP_STRIDE)
    tiles_per_chunk = KC // TQ
    _fill_masks(mask_c, lambda r, q: r <= q, lambda r, q: r - TQ <= q)
    _fill_masks(mask_w, lambda r, q: r <= q, lambda r, q: r > q)

    def qs_of(qi):
        t0 = pl.multiple_of(qi * TQ, TQ)
        return [qaug_ref[0, r, pl.ds(t0, TQ), :] for r in range(NSA_GROUP)]

    def slc_done(qi, accs):
        for r in range(NSA_GROUP):
            oslc_s[qi, r] = _normalise(accs[r])

    _flash_pipeline(
        tbl_ref, 0, n_slc_items, KC, NSA_GROUP, qs_of,
        lambda h, c: ks_ref[0, pl.ds(pl.multiple_of(c * KC, KC), KC), :],
        lambda h, c: [vt_ref[0, c * tiles_per_chunk + j, 0:VROWS, :] for j in range(tiles_per_chunk)],
        s_buf, mask_c, slc_done)

    def win_done(qi, accs):
        t0 = pl.multiple_of(qi * TQ, TQ)
        gates = gt_ref[0, qi, pl.ds(g_row0, GATE_GROUP_STRIDE), :]
        outs = []
        for r in range(NSA_GROUP):
            o_cmp = ocmp_ref[0, qi, r * HEAD_DIM:(r + 1) * HEAD_DIM, :]
            outs.append(gates[3 * r:3 * r + 1, :] * o_cmp + gates[3 * r + 1:3 * r + 2, :] * oslc_s[qi, r]
                        + gates[3 * r + 2:3 * r + 3, :] * _normalise(accs[r]))
        _store_gated(y_ref, zg_ref, outs, t0)

    _flash_pipeline(
        tbl_ref, ITEM_FIELDS, n_win_items, TQ, NSA_GROUP, qs_of,
        lambda h, c: kw_ref[0, pl.ds(pl.multiple_of(c * TQ, TQ), TQ), :],
        lambda h, c: [vt_ref[0, c, VROWS:2 * VROWS, :]],
        s_buf, mask_w, win_done)


def _nsa_attend(qaug, nks, nkw, nvt, ngt, nzg, ocmp_t):
    bsz, n_heads, seq, _ = qaug.shape
    gw = NSA_GROUP * HEAD_DIM
    nq_t = seq // TQ
    assert seq % KC == 0 and WINDOW % TQ == 0
    slc_items = _item_table(nq_t, KC)
    win_items = _item_table(nq_t, TQ, WINDOW // TQ)
    width = max(slc_items.shape[1], win_items.shape[1])
    table = np.zeros((2 * ITEM_FIELDS, width), np.int32)
    table[0:ITEM_FIELDS, :slc_items.shape[1]] = slc_items
    table[ITEM_FIELDS:, :win_items.shape[1]] = win_items
    grid_spec = pltpu.PrefetchScalarGridSpec(
        num_scalar_prefetch=1,
        grid=(bsz, NSA_KV_HEADS),
        in_specs=[pl.BlockSpec((1, NSA_GROUP, seq, LANES), lambda b, g, t: (b, g, 0, 0)),
                  pl.BlockSpec((1, seq, LANES), lambda b, g, t: (b, 0, g)),
                  pl.BlockSpec((1, seq, LANES), lambda b, g, t: (b, 0, g)),
                  pl.BlockSpec((1, nq_t, 2 * VROWS, TQ), lambda b, g, t: (b, 0, g, 0)),
                  pl.BlockSpec((1, nq_t, LANES, TQ), lambda b, g, t: (b, 0, 0, 0)),
                  pl.BlockSpec((1, seq, gw), lambda b, g, t: (b, 0, g)),
                  pl.BlockSpec((1, nq_t, gw, TQ), lambda b, g, t: (b, 0, g, 0))],
        out_specs=pl.BlockSpec((1, seq, gw), lambda b, g, t: (b, 0, g)),
        scratch_shapes=[pltpu.VMEM((2, NSA_GROUP, KC, TQ), jnp.float32),
                        pltpu.VMEM((4, KC, TQ), jnp.float32),
                        pltpu.VMEM((4, TQ, TQ), jnp.float32),
                        pltpu.VMEM((nq_t, NSA_GROUP, HEAD_DIM, TQ), jnp.float32)])
    return pl.pallas_call(
        functools.partial(_nsa_kernel, slc_items.shape[1], win_items.shape[1]),
        out_shape=jax.ShapeDtypeStruct((bsz, seq, n_heads * HEAD_DIM), jnp.bfloat16),
        grid_spec=grid_spec,
        compiler_params=_params(("parallel", "arbitrary")),
        name="nsa_attend",
    )(jnp.asarray(table), qaug, nks, nkw, nvt, ngt, nzg, ocmp_t)


def _fox_kernel(n_items, tbl_ref, q_ref, k_ref, ex_ref, vt_ref, zg_ref, y_ref, s_buf, mask_c):
    n_heads = q_ref.shape[2] // HEAD_DIM
    heads_per_blk = LANES // HEAD_DIM
    tiles_per_chunk = KC // TQ
    _fill_masks(mask_c, lambda r, q: r <= q, lambda r, q: r - TQ <= q)
    lane = lax.broadcasted_iota(jnp.int32, (TQ, LANES), 1)
    own = [(lane >= j * HEAD_DIM) & (lane < (j + 1) * HEAD_DIM) for j in range(heads_per_blk)]
    ones = [jnp.where((lane >= j * BIAS_TERMS) & (lane < (j + 1) * BIAS_TERMS), 1.0, 0.0).astype(jnp.bfloat16)
            for j in range(heads_per_blk)]

    def qs_of(qi):
        t0 = pl.multiple_of(qi * TQ, TQ)
        qs = []
        for blk in range(n_heads // heads_per_blk):
            pair = q_ref[0, pl.ds(t0, TQ), blk * LANES:(blk + 1) * LANES]
            for j in range(heads_per_blk):
                qs.append(jnp.concatenate([jnp.where(own[j], pair, jnp.zeros_like(pair)), ones[j]], axis=-1))
        return qs

    def k_of(h, c):
        k0 = pl.multiple_of(c * KC, KC)
        lanes = slice((h // heads_per_blk) * LANES, (h // heads_per_blk + 1) * LANES)
        return jnp.concatenate([k_ref[0, pl.ds(k0, KC), lanes], ex_ref[0, pl.ds(k0, KC), lanes]], axis=-1)

    def done(qi, accs):
        _store_gated(y_ref, zg_ref, [_normalise(a) for a in accs], pl.multiple_of(qi * TQ, TQ))

    _flash_pipeline(
        tbl_ref, 0, n_items, KC, n_heads, qs_of, k_of,
        lambda h, c: [vt_ref[0, c * tiles_per_chunk + j, h * VROWS:(h + 1) * VROWS, :] for j in range(tiles_per_chunk)],
        s_buf, mask_c, done)


def _fox_attend(fqn, fkn, fex, fvt, fzg):
    bsz, seq, width = fqn.shape
    bw = FOX_HEADS_PER_STEP * HEAD_DIM
    assert seq % KC == 0 and width % bw == 0
    table = _item_table(seq // TQ, KC)
    blk = pl.BlockSpec((1, seq, bw), lambda b, p, t: (b, 0, p))
    grid_spec = pltpu.PrefetchScalarGridSpec(
        num_scalar_prefetch=1,
        grid=(bsz, width // bw),
        in_specs=[blk, blk, blk,
                  pl.BlockSpec((1, seq // TQ, FOX_HEADS_PER_STEP * VROWS, TQ), lambda b, p, t: (b, 0, p, 0)),
                  blk],
        out_specs=blk,
        scratch_shapes=[pltpu.VMEM((2, FOX_HEADS_PER_STEP, KC, TQ), jnp.float32),
                        pltpu.VMEM((4, KC, TQ), jnp.float32)])
    return pl.pallas_call(
        functools.partial(_fox_kernel, table.shape[1]),
        out_shape=jax.ShapeDtypeStruct((bsz, seq, width), jnp.bfloat16),
        grid_spec=grid_spec,
        compiler_params=_params(("parallel", "arbitrary")),
        name="fox_attend",
    )(jnp.asarray(table), fqn, fkn, fex, fvt, fzg)


def _block_diag(blocks):
    n, r, c = blocks.shape
    eye = jnp.eye(n, dtype=blocks.dtype)
    return (eye[:, None, :, None] * blocks[:, :, None, :]).reshape(n * r, n * c)


def _layer_layout(d_model):
    lru_w = d_model // 2
    n_heads = d_model // 128
    aw = n_heads * HEAD_DIM
    kvw = NSA_KV_HEADS * HEAD_DIM
    splits = (lru_w, lru_w, aw, kvw, kvw, kvw, kvw, kvw, kvw, 3 * n_heads, aw, aw, aw, aw, n_heads, aw)
    offs = np.concatenate([[0], np.cumsum(splits)])
    names = ("lru_u", "lru_z", "nq", "kc", "vc", "ks", "vs", "kw", "vw", "gl", "nz", "fq", "fk", "fv", "fl", "fz")
    col = {n: np.arange(offs[i], offs[i + 1]) for i, n in enumerate(names)}
    nkv = []
    for g in range(NSA_KV_HEADS):
        for n in ("ks", "vs", "kw", "vw"):
            nkv.append(col[n][g * HEAD_DIM:(g + 1) * HEAD_DIM])
    misc = np.full((LANES,), -1, np.int64)
    per_group = 3 * NSA_GROUP
    for g in range(NSA_KV_HEADS):
        misc[g * GATE_GROUP_STRIDE:g * GATE_GROUP_STRIDE + per_group] = col["gl"][g * per_group:(g + 1) * per_group]
    misc[FORGET_LANE0:FORGET_LANE0 + n_heads] = col["fl"]
    segs = [("lru", np.concatenate([col["lru_u"], col["lru_z"]])),
            ("nq", col["nq"]),
            ("kc", col["kc"]), ("vc", col["vc"]),
            ("nkv", np.concatenate(nkv)),
            ("nz", col["nz"]),
            ("fq", col["fq"]), ("fk", col["fk"]), ("fv", col["fv"]), ("fz", col["fz"]),
            ("misc", misc)]
    return segs, n_heads


def _permute_w_in(w_in, segs):
    d = w_in.shape[0]
    w_ext = jnp.concatenate([w_in, jnp.zeros((d, 1), w_in.dtype)], axis=1)
    idx = np.concatenate([np.where(c < 0, w_in.shape[1], c) for _, c in segs])
    return w_ext[:, idx].astype(jnp.bfloat16)


def _hybrid_layer(x, norm_g, w_in, w_out, conv_w, conv_b, lru_wa, lru_ba, lru_wx, lru_bx, lru_lambda,
                  nsa_q_g, nsa_k_g, cmp_pe_k, cmp_pe_v, cmp_wk, cmp_wv, nsa_gate_b,
                  fox_q_g, fox_k_g, fox_f_b):
    bsz, seq, d_model = x.shape
    segs, n_heads = _layer_layout(d_model)
    aw = n_heads * HEAD_DIM
    heads_per_blk = LANES // HEAD_DIM
    assert seq % TQ == 0 and seq // CMP_STRIDE == LANES and n_heads * 3 <= 2 * GATE_GROUP_STRIDE
    assert NSA_KV_HEADS * 4 * HEAD_DIM == aw and BIAS_TERMS * heads_per_blk <= LANES
    seg, off = {}, 0
    for name, c in segs:
        seg[name] = (off, off + len(c))
        off += len(c)
    x2d = x.reshape(bsz * seq, d_model)

    ones_bd = _block_diag(jnp.ones((n_heads, HEAD_DIM, HEAD_DIM), jnp.bfloat16))
    one = jnp.ones((HEAD_DIM,), jnp.float32)
    nkv_gain = jnp.concatenate([nsa_k_g[1], one, nsa_k_g[2], one] * NSA_KV_HEADS)
    q_scale = ATTN_SCALE * LOG2E
    gain_rows = jnp.stack([jnp.tile(nsa_q_g, n_heads) * q_scale, nkv_gain,
                           jnp.tile(fox_q_g, n_heads) * q_scale, jnp.tile(fox_k_g, n_heads)])
    per_group = 3 * NSA_GROUP
    misc_bias = jnp.zeros((1, LANES), jnp.float32)
    for g in range(NSA_KV_HEADS):
        misc_bias = misc_bias.at[0, g * GATE_GROUP_STRIDE:g * GATE_GROUP_STRIDE + per_group].set(
            nsa_gate_b[g * per_group:(g + 1) * per_group])
    misc_bias = misc_bias.at[0, FORGET_LANE0:FORGET_LANE0 + n_heads].set(fox_f_b)
    place = np.zeros((BIAS_TERMS, LANES, aw), np.float32)
    for h in range(n_heads):
        for t in range(BIAS_TERMS):
            place[t, FORGET_LANE0 + h, (h // heads_per_blk) * LANES + (h % heads_per_blk) * BIAS_TERMS + t] = 1.0

    (lru, nqn, kc, vc, nks, nkw, nvt, ngt, nzg, fqn, fkn, fex, fvt, fzg) = _in_proj(
        x2d, norm_g, _permute_w_in(w_in, segs), seg, seq, ones_bd, gain_rows, misc_bias,
        jnp.asarray(place, jnp.bfloat16))
    tok = lambda a: a.reshape(bsz, seq, a.shape[1])
    tiled = lambda a: a.reshape(bsz, seq // TQ, a.shape[1], TQ)

    w_gates = jnp.concatenate([_block_diag(lru_wa), _block_diag(lru_wx)], axis=1).astype(jnp.bfloat16)
    y_lru = _lru(tok(lru), conv_w, conv_b, w_gates, jnp.concatenate([lru_ba, lru_bx]), lru_lambda)

    tile2 = lambda a: jnp.tile(a, (1, NSA_KV_HEADS))
    bd2 = lambda w: jax.vmap(lambda m: _block_diag(jnp.stack([m] * NSA_KV_HEADS)))(w).astype(jnp.bfloat16)
    qaug, ocmp_t = _nsa_cmp(tok(nqn), tok(kc), tok(vc), nsa_k_g[0], tile2(cmp_pe_k), tile2(cmp_pe_v),
                            bd2(cmp_wk), bd2(cmp_wv))
    y_nsa = _nsa_attend(qaug, tok(nks), tok(nkw), tiled(nvt), tiled(ngt), tok(nzg), ocmp_t)

    y_fox = _fox_attend(tok(fqn), tok(fkn), tok(fex), tiled(fvt), tok(fzg))

    flat = lambda a: a.reshape(bsz * seq, a.shape[2])
    out = _out_proj(x2d, flat(y_lru), flat(y_nsa), flat(y_fox), w_out.astype(jnp.bfloat16))
    return out.reshape(bsz, seq, d_model)


def kernel(x, norm_g, w_in, w_out, conv_w, conv_b, lru_wa, lru_ba, lru_wx, lru_bx, lru_lambda, nsa_q_g, nsa_k_g, cmp_pe_k, cmp_pe_v, cmp_wk, cmp_wv, nsa_gate_b, fox_q_g, fox_k_g, fox_f_b):
    for l in range(norm_g.shape[0]):
        x = _hybrid_layer(x, norm_g[l], w_in[l], w_out[l], conv_w[l], conv_b[l], lru_wa[l], lru_ba[l],
                          lru_wx[l], lru_bx[l], lru_lambda[l], nsa_q_g[l], nsa_k_g[l], cmp_pe_k[l],
                          cmp_pe_v[l], cmp_wk[l], cmp_wv[l], nsa_gate_b[l], fox_q_g[l], fox_k_g[l],
                          fox_f_b[l])
    return x
```

```python
import functools

import jax
import jax.numpy as jnp
import numpy as np
from jax import lax
from jax.experimental import pallas as pl
from jax.experimental.pallas import tpu as pltpu

HEAD_DIM = 64
LRU_BLOCKS = 8
CONV_WIDTH = 4
LRU_C = 8.0
NSA_KV_HEADS = 2
NSA_GROUP = 4
CMP_LEN = 32
CMP_STRIDE = 16
SLC_LEN = 64
N_SELECT = 16
WINDOW = 512
NORM_EPS = 1e-6
NEG_INF = -1e30
FORCE_SCORE = 1e9
ATTN_SCALE = HEAD_DIM ** -0.5
LOG2E = 1.4426950408889634

LANES = 128
SUBLANES = 8
TQ = 256
KC = 512
PROJ_ROWS = TQ
LRU_ROWS = 512
VMEM_LIMIT = 56 * 1024 * 1024

GATE_GROUP_STRIDE = 16
FORGET_LANE0 = 32
FOX_HEADS_PER_STEP = 4
BIAS_TERMS = 3
VROWS = 80

_NT = (((1,), (1,)), ((), ()))


def _dot(a, b):
    return jnp.dot(a, b, preferred_element_type=jnp.float32)


def _dot_nt(a, b):
    return lax.dot_general(a, b, _NT, preferred_element_type=jnp.float32)


def _split3(x):
    hi = x.astype(jnp.bfloat16)
    r = x - hi.astype(jnp.float32)
    mid = r.astype(jnp.bfloat16)
    lo = (r - mid.astype(jnp.float32)).astype(jnp.bfloat16)
    return hi, mid, lo


def _dot01_left(m01, x):
    hi, mid, lo = _split3(x)
    return _dot(m01, hi) + _dot(m01, mid) + _dot(m01, lo)


def _head_rmsnorm(x, gain):
    ms = jnp.sum(x * x, axis=-1, keepdims=True) * (1.0 / HEAD_DIM)
    return x * lax.rsqrt(ms + NORM_EPS) * gain


def _heads_rmsnorm(x, ones_bd, gain_row):
    ss = _dot((x * x).astype(jnp.bfloat16), ones_bd)
    return x * lax.rsqrt(ss * (1.0 / HEAD_DIM) + NORM_EPS) * gain_row


def _sigmoid(x):
    return 1.0 / (1.0 + jnp.exp(-x))


def _silu(x):
    return x * _sigmoid(x)


def _params(sem):
    return pltpu.CompilerParams(dimension_semantics=sem, vmem_limit_bytes=VMEM_LIMIT)


def _in_proj_kernel(seg, tiles_per_seq, x_ref, g_ref, w_ref, bd_ref, gain_ref, mb_ref, place_ref,
                    lru_ref, nqn_ref, kc_ref, vc_ref, nks_ref, nkw_ref, nvt_ref, ngt_ref, nzg_ref,
                    fqn_ref, fkn_ref, fex_ref, fvt_ref, fzg_ref, carry):
    i = pl.program_id(0)
    rows = x_ref.shape[0]
    x = x_ref[...]
    ms = jnp.mean(x * x, axis=-1, keepdims=True)
    h = (x * lax.rsqrt(ms + NORM_EPS) * g_ref[...]).astype(jnp.bfloat16)
    proj = lambda name: _dot(h, w_ref[:, seg[name][0]:seg[name][1]])
    bd = bd_ref[...]
    bf = jnp.bfloat16

    lru_ref[...] = proj("lru")
    kc_ref[...] = proj("kc")
    vc_ref[...] = proj("vc")
    nqn_ref[...] = _heads_rmsnorm(proj("nq"), bd, gain_ref[0:1, :]).astype(bf)
    nzg_ref[...] = _silu(proj("nz")).astype(bf)
    fqn_ref[...] = _heads_rmsnorm(proj("fq"), bd, gain_ref[2:3, :]).astype(bf)
    fkn_ref[...] = _heads_rmsnorm(proj("fk"), bd, gain_ref[3:4, :]).astype(bf)
    fzg_ref[...] = _silu(proj("fz")).astype(bf)
    pad_row = lax.broadcasted_iota(jnp.int32, (VROWS - HEAD_DIM, rows), 0)
    ones_pad = jnp.where(pad_row == 0, 1.0, 0.0).astype(bf)

    def store_vt(ref, slot, v_t):
        ref[0, slot * VROWS:slot * VROWS + HEAD_DIM, :] = v_t.astype(bf)
        ref[0, slot * VROWS + HEAD_DIM:(slot + 1) * VROWS, :] = ones_pad

    fv = proj("fv")
    for j in range(fv.shape[1] // LANES):
        fv_t = fv[:, j * LANES:(j + 1) * LANES].T
        store_vt(fvt_ref, 2 * j, fv_t[0:HEAD_DIM, :])
        store_vt(fvt_ref, 2 * j + 1, fv_t[HEAD_DIM:2 * HEAD_DIM, :])

    nkv = proj("nkv")
    nkv_n = _heads_rmsnorm(nkv, bd, gain_ref[1:2, :])
    lane = lax.broadcasted_iota(jnp.int32, (rows, LANES), 1)
    pos = (i % tiles_per_seq) * rows + lax.broadcasted_iota(jnp.int32, (rows, LANES), 0)
    onehot = jnp.where(lane - HEAD_DIM == pos // SLC_LEN, 1.0, 0.0)
    for g in range(NSA_KV_HEADS):
        b0 = 2 * g * LANES
        nks_ref[:, g * LANES:(g + 1) * LANES] = jnp.where(lane < HEAD_DIM, nkv_n[:, b0:b0 + LANES], onehot).astype(bf)
        nkw_ref[:, g * LANES:(g + 1) * LANES] = jnp.where(lane < HEAD_DIM, nkv_n[:, b0 + LANES:b0 + 2 * LANES],
                                                          0.0).astype(bf)
    for j in range(nkv.shape[1] // LANES):
        store_vt(nvt_ref, j, nkv[:, j * LANES:(j + 1) * LANES].T[HEAD_DIM:2 * HEAD_DIM, :])

    misc = proj("misc") + mb_ref[...]
    ngt_ref[0] = _sigmoid(misc).T
    log_f = jnp.minimum(misc, 0.0) - jnp.log1p(jnp.exp(-jnp.abs(misc)))

    @pl.when(i % tiles_per_seq == 0)
    def _():
        carry[...] = jnp.zeros_like(carry)

    tri = (lax.broadcasted_iota(jnp.int32, (rows, rows), 0)
           >= lax.broadcasted_iota(jnp.int32, (rows, rows), 1)).astype(bf)
    cum = _dot01_left(tri, log_f) + carry[...]
    carry[...] = cum[rows - 1:rows, :]
    terms = _split3(cum * (-LOG2E))
    fex = _dot(terms[0], place_ref[0])
    for t in range(1, BIAS_TERMS):
        fex = fex + _dot(terms[t], place_ref[t])
    fex_ref[...] = fex.astype(bf)


def _in_proj(x2d, norm_g, w_perm, seg, seq, ones_bd, gain_rows, misc_bias, place):
    n, d = x2d.shape
    aw = seg["nq"][1] - seg["nq"][0]
    rows = PROJ_ROWS
    assert n % rows == 0 and seq % rows == 0
    n_tiles = n // rows
    f32, bf = jnp.float32, jnp.bfloat16
    row_out = lambda w, dt: (jax.ShapeDtypeStruct((n, w), dt), pl.BlockSpec((rows, w), lambda i: (i, 0)))
    t_out = lambda w, dt: (jax.ShapeDtypeStruct((n_tiles, w, rows), dt), pl.BlockSpec((1, w, rows), lambda i: (i, 0, 0)))
    outs = [row_out(seg["lru"][1] - seg["lru"][0], f32),
            row_out(aw, bf),
            row_out(LANES, f32), row_out(LANES, f32),
            row_out(NSA_KV_HEADS * LANES, bf),
            row_out(NSA_KV_HEADS * LANES, bf),
            t_out(2 * NSA_KV_HEADS * VROWS, bf),
            t_out(LANES, f32),
            row_out(aw, bf),
            row_out(aw, bf), row_out(aw, bf), row_out(aw, bf),
            t_out(aw // HEAD_DIM * VROWS, bf),
            row_out(aw, bf)]
    const2 = lambda a: pl.BlockSpec(a.shape, lambda i: (0, 0))
    return pl.pallas_call(
        functools.partial(_in_proj_kernel, seg, seq // rows),
        out_shape=[o[0] for o in outs],
        grid=(n_tiles,),
        in_specs=[pl.BlockSpec((rows, d), lambda i: (i, 0)),
                  pl.BlockSpec((1, d), lambda i: (0, 0)),
                  const2(w_perm), const2(ones_bd), const2(gain_rows), const2(misc_bias),
                  pl.BlockSpec(place.shape, lambda i: (0, 0, 0))],
        out_specs=[o[1] for o in outs],
        scratch_shapes=[pltpu.VMEM((1, LANES), jnp.float32)],
        compiler_params=_params(("arbitrary",)),
        name="in_proj",
    )(x2d, norm_g.reshape(1, d), w_perm, ones_bd, gain_rows, misc_bias, place)


def _out_proj_kernel(x_ref, ya_ref, yb_ref, yc_ref, w_ref, o_ref):
    wa = ya_ref.shape[1]
    wb = yb_ref.shape[1]
    acc = x_ref[...]
    acc = acc + _dot(ya_ref[...], w_ref[0:wa, :])
    acc = acc + _dot(yb_ref[...], w_ref[wa:wa + wb, :])
    acc = acc + _dot(yc_ref[...], w_ref[wa + wb:, :])
    o_ref[...] = acc


def _out_proj(x2d, ya, yb, yc, w_out_bf16):
    n, d = x2d.shape
    row = lambda i: (i, 0)
    return pl.pallas_call(
        _out_proj_kernel,
        out_shape=jax.ShapeDtypeStruct((n, d), jnp.float32),
        grid=(n // PROJ_ROWS,),
        in_specs=[pl.BlockSpec((PROJ_ROWS, d), row),
                  pl.BlockSpec((PROJ_ROWS, ya.shape[1]), row),
                  pl.BlockSpec((PROJ_ROWS, yb.shape[1]), row),
                  pl.BlockSpec((PROJ_ROWS, yc.shape[1]), row),
                  pl.BlockSpec(w_out_bf16.shape, lambda i: (0, 0))],
        out_specs=pl.BlockSpec((PROJ_ROWS, d), row),
        compiler_params=_params(("parallel",)),
        name="out_proj",
    )(x2d, ya, yb, yc, w_out_bf16)


def _lru_kernel(uz_ref, cw_ref, cb_ref, wg_ref, bg_ref, lam_ref, y_ref, ubuf, a_s, b_s, hc):
    rows = a_s.shape[0]
    width = a_s.shape[1]

    @pl.when(pl.program_id(1) == 0)
    def _():
        ubuf[0:SUBLANES, :] = jnp.zeros((SUBLANES, width), jnp.float32)
        hc[...] = jnp.zeros_like(hc)

    u = uz_ref[0, :, 0:width]
    z = uz_ref[0, :, width:2 * width]
    ubuf[SUBLANES:SUBLANES + rows, :] = u
    xc = cb_ref[...] + cw_ref[CONV_WIDTH - 1:CONV_WIDTH, :] * u
    for k in range(CONV_WIDTH - 1):
        shift = CONV_WIDTH - 1 - k
        xc = xc + cw_ref[k:k + 1, :] * ubuf[SUBLANES - shift:SUBLANES - shift + rows, :]
    ubuf[0:SUBLANES, :] = u[rows - SUBLANES:rows, :]

    gates = _dot(xc.astype(jnp.bfloat16), wg_ref[...]) + bg_ref[...]
    r = _sigmoid(gates[:, 0:width])
    ig = _sigmoid(gates[:, width:2 * width])
    nlam = -lam_ref[...]
    softplus = jnp.maximum(nlam, 0.0) + jnp.log1p(jnp.exp(-jnp.abs(nlam)))
    log_a = (-LRU_C) * r * softplus
    a = jnp.exp(log_a)
    a_s[...] = a
    b_s[...] = jnp.sqrt(1.0 - a * a) * (ig * xc)

    row = lax.broadcasted_iota(jnp.int32, (SUBLANES, width), 0)

    def group(gi, h_prev):
        r0 = pl.multiple_of(gi * SUBLANES, SUBLANES)
        a8 = a_s[pl.ds(r0, SUBLANES), :]
        b8 = b_s[pl.ds(r0, SUBLANES), :]
        d = 1
        while d < SUBLANES:
            a_sh = jnp.where(row >= d, pltpu.roll(a8, d, 0), 1.0)
            b_sh = jnp.where(row >= d, pltpu.roll(b8, d, 0), 0.0)
            b8 = b8 + a8 * b_sh
            a8 = a8 * a_sh
            d *= 2
        h8 = b8 + a8 * h_prev
        b_s[pl.ds(r0, SUBLANES), :] = h8
        return h8[SUBLANES - 1:SUBLANES, :]

    hc[...] = lax.fori_loop(0, rows // SUBLANES, group, hc[...])
    y_ref[0] = (b_s[...] * _silu(z)).astype(y_ref.dtype)


def _lru(uz, conv_w, conv_b, w_gates, b_gates, lam):
    bsz, seq, w2 = uz.shape
    width = w2 // 2
    rows = min(LRU_ROWS, seq)
    assert seq % rows == 0
    const = lambda b, i: (0, 0)
    return pl.pallas_call(
        _lru_kernel,
        out_shape=jax.ShapeDtypeStruct((bsz, seq, width), jnp.bfloat16),
        grid=(bsz, seq // rows),
        in_specs=[pl.BlockSpec((1, rows, w2), lambda b, i: (b, i, 0)),
                  pl.BlockSpec(conv_w.shape, const),
                  pl.BlockSpec((1, width), const),
                  pl.BlockSpec(w_gates.shape, const),
                  pl.BlockSpec((1, w2), const),
                  pl.BlockSpec((1, width), const)],
        out_specs=pl.BlockSpec((1, rows, width), lambda b, i: (b, i, 0)),
        scratch_shapes=[pltpu.VMEM((rows + SUBLANES, width), jnp.float32),
                        pltpu.VMEM((rows, width), jnp.float32),
                        pltpu.VMEM((rows, width), jnp.float32),
                        pltpu.VMEM((1, width), jnp.float32)],
        compiler_params=_params(("parallel", "arbitrary")),
        name="rglru",
    )(uz, conv_w, conv_b.reshape(1, width), w_gates, b_gates.reshape(1, w2), lam.reshape(1, width))


def _cmp_kernel(q_ref, kc_ref, vc_ref, kg_ref, pek_ref, pev_ref, wk_ref, wv_ref, qaug_ref, ocmp_ref):
    seq = q_ref.shape[1]
    n_blk = seq // CMP_STRIDE
    n_cmp = (seq - CMP_LEN) // CMP_STRIDE + 1
    n_slc = seq // SLC_LEN
    half = CMP_LEN // CMP_STRIDE
    assert half == 2 and n_blk == LANES
    kvw = NSA_KV_HEADS * HEAD_DIM
    heads_per_blk = LANES // HEAD_DIM

    acc = [jnp.zeros((n_blk, kvw), jnp.float32) for _ in range(4)]
    for l in range(CMP_STRIDE):
        xk = kc_ref[0, pl.ds(l, n_blk, stride=CMP_STRIDE), :]
        xv = vc_ref[0, pl.ds(l, n_blk, stride=CMP_STRIDE), :]
        l2 = CMP_STRIDE + l
        acc[0] = acc[0] + _dot((xk + pek_ref[l:l + 1, :]).astype(jnp.bfloat16), wk_ref[l])
        acc[1] = acc[1] + _dot((xk + pek_ref[l2:l2 + 1, :]).astype(jnp.bfloat16), wk_ref[l2])
        acc[2] = acc[2] + _dot((xv + pev_ref[l:l + 1, :]).astype(jnp.bfloat16), wv_ref[l])
        acc[3] = acc[3] + _dot((xv + pev_ref[l2:l2 + 1, :]).astype(jnp.bfloat16), wv_ref[l2])
    k_cmp = acc[0] + pltpu.roll(acc[1], n_blk - 1, 0)
    v_cmp = acc[2] + pltpu.roll(acc[3], n_blk - 1, 0)
    v_cmp_t = v_cmp.T.astype(jnp.bfloat16)
    k_n = [_head_rmsnorm(k_cmp[:, g * HEAD_DIM:(g + 1) * HEAD_DIM], kg_ref[...]).astype(jnp.bfloat16)
           for g in range(NSA_KV_HEADS)]
    v_t = [v_cmp_t[g * HEAD_DIM:(g + 1) * HEAD_DIM, :] for g in range(NSA_KV_HEADS)]

    n_idx = lax.broadcasted_iota(jnp.int32, (n_blk, TQ), 0)
    j_idx = lax.broadcasted_iota(jnp.int32, (n_slc, TQ), 0)
    oj = lax.broadcasted_iota(jnp.int32, (n_slc, n_blk), 0) * SLC_LEN
    on = lax.broadcasted_iota(jnp.int32, (n_slc, n_blk), 1) * CMP_STRIDE
    overlap_t = ((on <= oj + SLC_LEN - 1) & (on + CMP_LEN - 1 >= oj)
                 & (on < n_cmp * CMP_STRIDE)).astype(jnp.bfloat16)

    def q_tile(qi, carry):
        t0 = pl.multiple_of(qi * TQ, TQ)
        t_row = t0 + lax.broadcasted_iota(jnp.int32, (1, TQ), 1)
        valid_cmp = (n_idx * CMP_STRIDE + CMP_LEN - 1 <= t_row) & (n_idx < n_cmp)
        blk_t = t_row // SLC_LEN
        valid_slc = j_idx <= blk_t
        forced = (j_idx == 0) | (j_idx == blk_t) | (j_idx == blk_t - 1)
        for g in range(NSA_KV_HEADS):
            q_heads = []
            for r in range(NSA_GROUP):
                h = g * NSA_GROUP + r
                blk = h // heads_per_blk
                pair = q_ref[0, pl.ds(t0, TQ), blk * LANES:(blk + 1) * LANES].astype(jnp.float32)
                j = h % heads_per_blk
                q_heads.append(pair[:, j * HEAD_DIM:(j + 1) * HEAD_DIM].astype(jnp.bfloat16))
            ss = [jnp.where(valid_cmp, _dot_nt(k_n[g], qh), NEG_INF) for qh in q_heads]
            es = [jnp.where(valid_cmp, jnp.exp2(s - jnp.max(s, axis=0, keepdims=True)), 0.0)
                  for s in ss]
            ps = [e / jnp.maximum(jnp.sum(e, axis=0, keepdims=True), 1e-30) for e in es]
            for r in range(NSA_GROUP):
                h = g * NSA_GROUP + r
                ocmp_ref[0, qi, h * HEAD_DIM:(h + 1) * HEAD_DIM, :] = _dot(v_t[g], ps[r].astype(jnp.bfloat16))
            p_sum = ps[0] + ps[1] + ps[2] + ps[3]
            imp = _dot01_left(overlap_t, p_sum)
            score = jnp.where(valid_slc, jnp.where(forced, FORCE_SCORE, imp), NEG_INF)
            rank = jnp.zeros((n_slc, TQ), jnp.float32)
            for jp in range(n_slc):
                other = score[jp:jp + 1, :]
                ahead = (other > score) | ((other == score) & (j_idx > jp))
                rank = rank + jnp.where(ahead, 1.0, 0.0)
            bias_t = jnp.where((rank < N_SELECT) & valid_slc, 0.0, NEG_INF)
            bias_t = jnp.concatenate([bias_t, jnp.zeros((LANES - n_slc, TQ), jnp.float32)], axis=0)
            bias = bias_t.T[:, 0:HEAD_DIM].astype(jnp.bfloat16)
            for r in range(NSA_GROUP):
                qaug_ref[0, g * NSA_GROUP + r, pl.ds(t0, TQ), :] = jnp.concatenate([q_heads[r], bias], axis=-1)
        return carry

    lax.fori_loop(0, seq // TQ, q_tile, 0)


def _nsa_cmp(nqn, kc, vc, k_g0, pe_k2, pe_v2, wk_bd, wv_bd):
    bsz, seq, qw = nqn.shape
    n_heads = qw // HEAD_DIM
    nq_t = seq // TQ
    assert seq // SLC_LEN <= HEAD_DIM
    c2 = lambda b: (0, 0)
    c3 = lambda b: (0, 0, 0)
    return pl.pallas_call(
        _cmp_kernel,
        out_shape=[jax.ShapeDtypeStruct((bsz, n_heads, seq, LANES), jnp.bfloat16),
                   jax.ShapeDtypeStruct((bsz, nq_t, qw, TQ), jnp.float32)],
        grid=(bsz,),
        in_specs=[pl.BlockSpec((1, seq, qw), lambda b: (b, 0, 0)),
                  pl.BlockSpec((1, seq, kc.shape[2]), lambda b: (b, 0, 0)),
                  pl.BlockSpec((1, seq, vc.shape[2]), lambda b: (b, 0, 0)),
                  pl.BlockSpec((1, HEAD_DIM), c2),
                  pl.BlockSpec(pe_k2.shape, c2),
                  pl.BlockSpec(pe_v2.shape, c2),
                  pl.BlockSpec(wk_bd.shape, c3),
                  pl.BlockSpec(wv_bd.shape, c3)],
        out_specs=[pl.BlockSpec((1, n_heads, seq, LANES), lambda b: (b, 0, 0, 0)),
                   pl.BlockSpec((1, nq_t, qw, TQ), lambda b: (b, 0, 0, 0))],
        compiler_params=_params(("parallel",)),
        name="nsa_compress",
    )(nqn, kc, vc, k_g0.reshape(1, HEAD_DIM), pe_k2, pe_v2, wk_bd, wv_bd)


ITEM_FIELDS = 5
def _item_table(n_q_tiles, mask_specs, window=None):
    def mask_id(spec):
        if spec not in mask_specs:
            mask_specs.append(spec)
        return mask_specs.index(spec)

    items = []
    for qi in range(n_q_tiles):
        t0 = qi * TQ
        first_chunk = 0 if window is None else max(t0 - window + 1, 0) // KC
        chunks = list(range(first_chunk, (t0 + TQ - 1) // KC + 1))
        for n, c in enumerate(chunks):
            needs_causal = c * KC + KC - 1 > t0
            needs_edge = window is not None and c * KC <= t0 + TQ - 1 - window
            assert not (needs_causal and needs_edge)
            spec = ("causal", t0 - c * KC) if needs_causal else (
                ("edge", t0 - c * KC - window) if needs_edge else ("none", 0))
            items.append((qi, c, int(n == 0), int(n == len(chunks) - 1), mask_id(spec)))
    if len(items) % 2:
        items.append((items[-1][0], items[-1][1], 0, 0, mask_id(("all", 0))))
    return np.asarray(items, np.int32).T


def _fill_masks(mask_s, mask_specs):
    n = mask_s.shape[1]
    r = lax.broadcasted_iota(jnp.int32, (n, TQ), 0)
    q = lax.broadcasted_iota(jnp.int32, (n, TQ), 1)
    for i, (kind, off) in enumerate(mask_specs):
        keep = {"none": r >= 0, "all": r < 0, "causal": r <= q + off, "edge": r > q + off}[kind]
        mask_s[i] = jnp.where(keep, 0.0, NEG_INF)


def _flash_pipeline(tbl_ref, row0, n_items, n_keys, n_heads, qs_of, k_of, vt_of, s_buf, mask_s, on_last):
    heads = range(n_heads)

    def produce(t, slot, h, qs, bias):
        s = _dot_nt(k_of(h, tbl_ref[row0 + 1, t]), qs[h]) + bias
        s_buf[slot, h, 0:n_keys, :] = s
        return jnp.max(s, axis=0, keepdims=True)

    def item(t, slot, carry):
        states, col_max = carry
        qi = tbl_ref[row0, t]
        chunk = tbl_ref[row0 + 1, t]
        first = tbl_ref[row0 + 2, t] == 1
        last = tbl_ref[row0 + 3, t]
        t_next = jnp.minimum(t + 1, n_items - 1)
        qs_next = qs_of(tbl_ref[row0, t_next])
        bias_next = mask_s[tbl_ref[row0 + 4, t_next]]
        out, col_max_next = [], []
        for h in heads:
            col_max_next.append(produce(t_next, 1 - slot, h, qs_next, bias_next))
            m_prev = jnp.where(first, NEG_INF, states[h][0])
            m_new = jnp.maximum(m_prev, col_max[h])
            p = jnp.exp2(s_buf[slot, h, 0:n_keys, :] - m_new).astype(jnp.bfloat16)
            acc = jnp.exp2(m_prev - m_new) * jnp.where(first, 0.0, states[h][1])
            for j, v_t in enumerate(vt_of(h, chunk)):
                acc = acc + _dot(v_t, p[j * TQ:(j + 1) * TQ, :])
            out.append((m_new, acc))

        @pl.when(last == 1)
        def _():
            on_last(qi, [o[1] for o in out])

        return tuple(out), tuple(col_max_next)

    qs0 = qs_of(tbl_ref[row0, 0])
    bias0 = mask_s[tbl_ref[row0 + 4, 0]]
    col_max0 = tuple(produce(0, 0, h, qs0, bias0) for h in heads)
    init = tuple((jnp.full((1, TQ), NEG_INF, jnp.float32), jnp.zeros((VROWS, TQ), jnp.float32)) for _ in heads)
    lax.fori_loop(0, n_items // 2, lambda i, c: item(2 * i + 1, 1, item(2 * i, 0, c)), (init, col_max0))


def _normalise(acc):
    return acc[0:HEAD_DIM, :] / jnp.maximum(acc[HEAD_DIM:HEAD_DIM + 1, :], 1e-30)


def _store_gated(y_ref, zg_ref, outs, t0):
    for pair in range(len(outs) // 2):
        o2 = jnp.concatenate(outs[2 * pair:2 * pair + 2], axis=0).T
        lanes = slice(pair * LANES, (pair + 1) * LANES)
        zg = zg_ref[0, pl.ds(t0, TQ), lanes].astype(jnp.float32)
        y_ref[0, pl.ds(t0, TQ), lanes] = (o2 * zg).astype(y_ref.dtype)


def _nsa_kernel(n_slc_items, n_win_items, mask_specs, tbl_ref, qaug_ref, ks_ref, kw_ref, vt_ref, gt_ref, zg_ref,
                ocmp_ref, y_ref, s_buf, mask_s, oslc_s, owin_s):
    g_id = pl.program_id(1)
    g_row0 = pl.multiple_of(g_id * GATE_GROUP_STRIDE, GATE_GROUP_STRIDE)
    tiles_per_chunk = KC // TQ
    _fill_masks(mask_s, mask_specs)

    def qs_of(qi):
        t0 = pl.multiple_of(qi * TQ, TQ)
        return [qaug_ref[0, r, pl.ds(t0, TQ), :] for r in range(NSA_GROUP)]

    def branch(row0, n_items, k_ref, v_rows, out_s):
        def done(qi, accs):
            for r in range(NSA_GROUP):
                out_s[qi, r] = _normalise(accs[r])

        _flash_pipeline(
            tbl_ref, row0, n_items, KC, NSA_GROUP, qs_of,
            lambda h, c: k_ref[0, pl.ds(pl.multiple_of(c * KC, KC), KC), :],
            lambda h, c: [vt_ref[0, c * tiles_per_chunk + j, v_rows, :] for j in range(tiles_per_chunk)],
            s_buf, mask_s, done)

    branch(0, n_slc_items, ks_ref, slice(0, VROWS), oslc_s)
    branch(ITEM_FIELDS, n_win_items, kw_ref, slice(VROWS, 2 * VROWS), owin_s)

    def combine(qi, carry):
        gates = gt_ref[0, qi, pl.ds(g_row0, GATE_GROUP_STRIDE), :]
        outs = []
        for r in range(NSA_GROUP):
            o_cmp = ocmp_ref[0, qi, r * HEAD_DIM:(r + 1) * HEAD_DIM, :]
            outs.append(gates[3 * r:3 * r + 1, :] * o_cmp + gates[3 * r + 1:3 * r + 2, :] * oslc_s[qi, r]
                        + gates[3 * r + 2:3 * r + 3, :] * owin_s[qi, r])
        _store_gated(y_ref, zg_ref, outs, pl.multiple_of(qi * TQ, TQ))
        return carry

    lax.fori_loop(0, oslc_s.shape[0], combine, 0)


def _nsa_attend(qaug, nks, nkw, nvt, ngt, nzg, ocmp_t):
    bsz, n_heads, seq, _ = qaug.shape
    gw = NSA_GROUP * HEAD_DIM
    nq_t = seq // TQ
    assert seq % KC == 0 and WINDOW % TQ == 0
    mask_specs = []
    slc_items = _item_table(nq_t, mask_specs)
    win_items = _item_table(nq_t, mask_specs, WINDOW)
    width = max(slc_items.shape[1], win_items.shape[1])
    table = np.zeros((2 * ITEM_FIELDS, width), np.int32)
    table[0:ITEM_FIELDS, :slc_items.shape[1]] = slc_items
    table[ITEM_FIELDS:, :win_items.shape[1]] = win_items
    grid_spec = pltpu.PrefetchScalarGridSpec(
        num_scalar_prefetch=1,
        grid=(bsz, NSA_KV_HEADS),
        in_specs=[pl.BlockSpec((1, NSA_GROUP, seq, LANES), lambda b, g, t: (b, g, 0, 0)),
                  pl.BlockSpec((1, seq, LANES), lambda b, g, t: (b, 0, g)),
                  pl.BlockSpec((1, seq, LANES), lambda b, g, t: (b, 0, g)),
                  pl.BlockSpec((1, nq_t, 2 * VROWS, TQ), lambda b, g, t: (b, 0, g, 0)),
                  pl.BlockSpec((1, nq_t, LANES, TQ), lambda b, g, t: (b, 0, 0, 0)),
                  pl.BlockSpec((1, seq, gw), lambda b, g, t: (b, 0, g)),
                  pl.BlockSpec((1, nq_t, gw, TQ), lambda b, g, t: (b, 0, g, 0))],
        out_specs=pl.BlockSpec((1, seq, gw), lambda b, g, t: (b, 0, g)),
        scratch_shapes=[pltpu.VMEM((2, NSA_GROUP, KC, TQ), jnp.float32),
                        pltpu.VMEM((len(mask_specs), KC, TQ), jnp.float32),
                        pltpu.VMEM((nq_t, NSA_GROUP, HEAD_DIM, TQ), jnp.float32),
                        pltpu.VMEM((nq_t, NSA_GROUP, HEAD_DIM, TQ), jnp.float32)])
    return pl.pallas_call(
        functools.partial(_nsa_kernel, slc_items.shape[1], win_items.shape[1], tuple(mask_specs)),
        out_shape=jax.ShapeDtypeStruct((bsz, seq, n_heads * HEAD_DIM), jnp.bfloat16),
        grid_spec=grid_spec,
        compiler_params=_params(("parallel", "arbitrary")),
        name="nsa_attend",
    )(jnp.asarray(table), qaug, nks, nkw, nvt, ngt, nzg, ocmp_t)


def _fox_kernel(n_items, mask_specs, tbl_ref, q_ref, k_ref, ex_ref, vt_ref, zg_ref, y_ref, s_buf, mask_s, o_s):
    n_heads = q_ref.shape[2] // HEAD_DIM
    heads_per_blk = LANES // HEAD_DIM
    tiles_per_chunk = KC // TQ
    _fill_masks(mask_s, mask_specs)
    lane = lax.broadcasted_iota(jnp.int32, (TQ, LANES), 1)
    own = [(lane >= j * HEAD_DIM) & (lane < (j + 1) * HEAD_DIM) for j in range(heads_per_blk)]
    ones = [jnp.where((lane >= j * BIAS_TERMS) & (lane < (j + 1) * BIAS_TERMS), 1.0, 0.0).astype(jnp.bfloat16)
            for j in range(heads_per_blk)]

    def qs_of(qi):
        t0 = pl.multiple_of(qi * TQ, TQ)
        qs = []
        for blk in range(n_heads // heads_per_blk):
            pair = q_ref[0, pl.ds(t0, TQ), blk * LANES:(blk + 1) * LANES]
            for j in range(heads_per_blk):
                qs.append(jnp.concatenate([jnp.where(own[j], pair, jnp.zeros_like(pair)), ones[j]], axis=-1))
        return qs

    def k_of(h, c):
        k0 = pl.multiple_of(c * KC, KC)
        lanes = slice((h // heads_per_blk) * LANES, (h // heads_per_blk + 1) * LANES)
        return jnp.concatenate([k_ref[0, pl.ds(k0, KC), lanes], ex_ref[0, pl.ds(k0, KC), lanes]], axis=-1)

    def done(qi, accs):
        for h in range(n_heads):
            o_s[qi, h] = _normalise(accs[h])

    _flash_pipeline(
        tbl_ref, 0, n_items, KC, n_heads, qs_of, k_of,
        lambda h, c: [vt_ref[0, c * tiles_per_chunk + j, h * VROWS:(h + 1) * VROWS, :] for j in range(tiles_per_chunk)],
        s_buf, mask_s, done)

    def finish(qi, carry):
        _store_gated(y_ref, zg_ref, [o_s[qi, h] for h in range(n_heads)], pl.multiple_of(qi * TQ, TQ))
        return carry

    lax.fori_loop(0, o_s.shape[0], finish, 0)


def _fox_attend(fqn, fkn, fex, fvt, fzg):
    bsz, seq, width = fqn.shape
    bw = FOX_HEADS_PER_STEP * HEAD_DIM
    assert seq % KC == 0 and width % bw == 0
    mask_specs = []
    table = _item_table(seq // TQ, mask_specs)
    blk = pl.BlockSpec((1, seq, bw), lambda b, p, t: (b, 0, p))
    grid_spec = pltpu.PrefetchScalarGridSpec(
        num_scalar_prefetch=1,
        grid=(bsz, width // bw),
        in_specs=[blk, blk, blk,
                  pl.BlockSpec((1, seq // TQ, FOX_HEADS_PER_STEP * VROWS, TQ), lambda b, p, t: (b, 0, p, 0)),
                  blk],
        out_specs=blk,
        scratch_shapes=[pltpu.VMEM((2, FOX_HEADS_PER_STEP, KC, TQ), jnp.float32),
                        pltpu.VMEM((len(mask_specs), KC, TQ), jnp.float32),
                        pltpu.VMEM((seq // TQ, FOX_HEADS_PER_STEP, HEAD_DIM, TQ), jnp.float32)])
    return pl.pallas_call(
        functools.partial(_fox_kernel, table.shape[1], tuple(mask_specs)),
        out_shape=jax.ShapeDtypeStruct((bsz, seq, width), jnp.bfloat16),
        grid_spec=grid_spec,
        compiler_params=_params(("parallel", "arbitrary")),
        name="fox_attend",
    )(jnp.asarray(table), fqn, fkn, fex, fvt, fzg)


def _block_diag(blocks):
    n, r, c = blocks.shape
    eye = jnp.eye(n, dtype=blocks.dtype)
    return (eye[:, None, :, None] * blocks[:, :, None, :]).reshape(n * r, n * c)


def _layer_layout(d_model):
    lru_w = d_model // 2
    n_heads = d_model // 128
    aw = n_heads * HEAD_DIM
    kvw = NSA_KV_HEADS * HEAD_DIM
    splits = (lru_w, lru_w, aw, kvw, kvw, kvw, kvw, kvw, kvw, 3 * n_heads, aw, aw, aw, aw, n_heads, aw)
    offs = np.concatenate([[0], np.cumsum(splits)])
    names = ("lru_u", "lru_z", "nq", "kc", "vc", "ks", "vs", "kw", "vw", "gl", "nz", "fq", "fk", "fv", "fl", "fz")
    col = {n: np.arange(offs[i], offs[i + 1]) for i, n in enumerate(names)}
    nkv = []
    for g in range(NSA_KV_HEADS):
        for n in ("ks", "vs", "kw", "vw"):
            nkv.append(col[n][g * HEAD_DIM:(g + 1) * HEAD_DIM])
    misc = np.full((LANES,), -1, np.int64)
    per_group = 3 * NSA_GROUP
    for g in range(NSA_KV_HEADS):
        misc[g * GATE_GROUP_STRIDE:g * GATE_GROUP_STRIDE + per_group] = col["gl"][g * per_group:(g + 1) * per_group]
    misc[FORGET_LANE0:FORGET_LANE0 + n_heads] = col["fl"]
    segs = [("lru", np.concatenate([col["lru_u"], col["lru_z"]])),
            ("nq", col["nq"]),
            ("kc", col["kc"]), ("vc", col["vc"]),
            ("nkv", np.concatenate(nkv)),
            ("nz", col["nz"]),
            ("fq", col["fq"]), ("fk", col["fk"]), ("fv", col["fv"]), ("fz", col["fz"]),
            ("misc", misc)]
    return segs, n_heads


def _permute_w_in(w_in, segs):
    d = w_in.shape[0]
    w_ext = jnp.concatenate([w_in, jnp.zeros((d, 1), w_in.dtype)], axis=1)
    idx = np.concatenate([np.where(c < 0, w_in.shape[1], c) for _, c in segs])
    return w_ext[:, idx].astype(jnp.bfloat16)


def _hybrid_layer(x, norm_g, w_in, w_out, conv_w, conv_b, lru_wa, lru_ba, lru_wx, lru_bx, lru_lambda,
                  nsa_q_g, nsa_k_g, cmp_pe_k, cmp_pe_v, cmp_wk, cmp_wv, nsa_gate_b,
                  fox_q_g, fox_k_g, fox_f_b):
    bsz, seq, d_model = x.shape
    segs, n_heads = _layer_layout(d_model)
    aw = n_heads * HEAD_DIM
    heads_per_blk = LANES // HEAD_DIM
    assert seq % TQ == 0 and seq // CMP_STRIDE == LANES and n_heads * 3 <= 2 * GATE_GROUP_STRIDE
    assert NSA_KV_HEADS * 4 * HEAD_DIM == aw and BIAS_TERMS * heads_per_blk <= LANES
    seg, off = {}, 0
    for name, c in segs:
        seg[name] = (off, off + len(c))
        off += len(c)
    x2d = x.reshape(bsz * seq, d_model)

    ones_bd = _block_diag(jnp.ones((n_heads, HEAD_DIM, HEAD_DIM), jnp.bfloat16))
    one = jnp.ones((HEAD_DIM,), jnp.float32)
    nkv_gain = jnp.concatenate([nsa_k_g[1], one, nsa_k_g[2], one] * NSA_KV_HEADS)
    q_scale = ATTN_SCALE * LOG2E
    gain_rows = jnp.stack([jnp.tile(nsa_q_g, n_heads) * q_scale, nkv_gain,
                           jnp.tile(fox_q_g, n_heads) * q_scale, jnp.tile(fox_k_g, n_heads)])
    per_group = 3 * NSA_GROUP
    misc_bias = jnp.zeros((1, LANES), jnp.float32)
    for g in range(NSA_KV_HEADS):
        misc_bias = misc_bias.at[0, g * GATE_GROUP_STRIDE:g * GATE_GROUP_STRIDE + per_group].set(
            nsa_gate_b[g * per_group:(g + 1) * per_group])
    misc_bias = misc_bias.at[0, FORGET_LANE0:FORGET_LANE0 + n_heads].set(fox_f_b)
    place = np.zeros((BIAS_TERMS, LANES, aw), np.float32)
    for h in range(n_heads):
        for t in range(BIAS_TERMS):
            place[t, FORGET_LANE0 + h, (h // heads_per_blk) * LANES + (h % heads_per_blk) * BIAS_TERMS + t] = 1.0

    (lru, nqn, kc, vc, nks, nkw, nvt, ngt, nzg, fqn, fkn, fex, fvt, fzg) = _in_proj(
        x2d, norm_g, _permute_w_in(w_in, segs), seg, seq, ones_bd, gain_rows, misc_bias,
        jnp.asarray(place, jnp.bfloat16))
    tok = lambda a: a.reshape(bsz, seq, a.shape[1])
    tiled = lambda a: a.reshape(bsz, seq // TQ, a.shape[1], TQ)

    w_gates = jnp.concatenate([_block_diag(lru_wa), _block_diag(lru_wx)], axis=1).astype(jnp.bfloat16)
    y_lru = _lru(tok(lru), conv_w, conv_b, w_gates, jnp.concatenate([lru_ba, lru_bx]), lru_lambda)

    tile2 = lambda a: jnp.tile(a, (1, NSA_KV_HEADS))
    bd2 = lambda w: jax.vmap(lambda m: _block_diag(jnp.stack([m] * NSA_KV_HEADS)))(w).astype(jnp.bfloat16)
    qaug, ocmp_t = _nsa_cmp(tok(nqn), tok(kc), tok(vc), nsa_k_g[0], tile2(cmp_pe_k), tile2(cmp_pe_v),
                            bd2(cmp_wk), bd2(cmp_wv))
    y_nsa = _nsa_attend(qaug, tok(nks), tok(nkw), tiled(nvt), tiled(ngt), tok(nzg), ocmp_t)

    y_fox = _fox_attend(tok(fqn), tok(fkn), tok(fex), tiled(fvt), tok(fzg))

    flat = lambda a: a.reshape(bsz * seq, a.shape[2])
    out = _out_proj(x2d, flat(y_lru), flat(y_nsa), flat(y_fox), w_out.astype(jnp.bfloat16))
    return out.reshape(bsz, seq, d_model)


def kernel(x, norm_g, w_in, w_out, conv_w, conv_b, lru_wa, lru_ba, lru_wx, lru_bx, lru_lambda, nsa_q_g, nsa_k_g, cmp_pe_k, cmp_pe_v, cmp_wk, cmp_wv, nsa_gate_b, fox_q_g, fox_k_g, fox_f_b):
    for l in range(norm_g.shape[0]):
        x = _hybrid_layer(x, norm_g[l], w_in[l], w_out[l], conv_w[l], conv_b[l], lru_wa[l], lru_ba[l],
                          lru_wx[l], lru_bx[l], lru_lambda[l], nsa_q_g[l], nsa_k_g[l], cmp_pe_k[l],
                          cmp_pe_v[l], cmp_wk[l], cmp_wv[l], nsa_gate_b[l], fox_q_g[l], fox_k_g[l],
                          fox_f_b[l])
    return x
```

```python
import functools

import jax
import jax.numpy as jnp
import numpy as np
from jax import lax
from jax.experimental import pallas as pl
from jax.experimental.pallas import tpu as pltpu

HEAD_DIM = 64
LRU_BLOCKS = 8
CONV_WIDTH = 4
LRU_C = 8.0
NSA_KV_HEADS = 2
NSA_GROUP = 4
CMP_LEN = 32
CMP_STRIDE = 16
SLC_LEN = 64
N_SELECT = 16
WINDOW = 512
NORM_EPS = 1e-6
NEG_INF = -1e30
FORCE_SCORE = 1e9
ATTN_SCALE = HEAD_DIM ** -0.5
LOG2E = 1.4426950408889634

LANES = 128
SUBLANES = 8
TQ = 256
KC = 512
PROJ_ROWS = TQ
OUT_ROWS = 512
VMEM_LIMIT = 56 * 1024 * 1024

GATE_GROUP_STRIDE = 16
FORGET_LANE0 = 32
FOX_HEADS_PER_STEP = 4
BIAS_TERMS = 3
VROWS = 80

_NT = (((1,), (1,)), ((), ()))


def _dot(a, b):
    return jnp.dot(a, b, preferred_element_type=jnp.float32)


def _dot_nt(a, b):
    return lax.dot_general(a, b, _NT, preferred_element_type=jnp.float32)


def _split3(x):
    hi = x.astype(jnp.bfloat16)
    r = x - hi.astype(jnp.float32)
    mid = r.astype(jnp.bfloat16)
    lo = (r - mid.astype(jnp.float32)).astype(jnp.bfloat16)
    return hi, mid, lo


def _dot01_left(m01, x):
    hi, mid, lo = _split3(x)
    return _dot(m01, hi) + _dot(m01, mid) + _dot(m01, lo)


def _head_rmsnorm(x, gain):
    ms = jnp.sum(x * x, axis=-1, keepdims=True) * (1.0 / HEAD_DIM)
    return x * lax.rsqrt(ms + NORM_EPS) * gain


def _heads_rmsnorm(x, ones_bd, gain_row):
    ss = _dot((x * x).astype(jnp.bfloat16), ones_bd)
    return x * lax.rsqrt(ss * (1.0 / HEAD_DIM) + NORM_EPS) * gain_row


def _sigmoid(x):
    return 1.0 / (1.0 + jnp.exp(-x))


def _silu(x):
    return x * _sigmoid(x)


def _params(sem):
    return pltpu.CompilerParams(dimension_semantics=sem, vmem_limit_bytes=VMEM_LIMIT)


def _in_proj_kernel(seg, tiles_per_seq, x_ref, g_ref, w_ref, bd_ref, gain_ref, mb_ref, place_ref,
                    cw_ref, cb_ref, wg_ref, bg_ref, lam_ref,
                    ylru_ref, nqn_ref, kc_ref, vc_ref, nks_ref, nkw_ref, nvt_ref, ngt_ref, nzg_ref,
                    fqn_ref, fkn_ref, fex_ref, fvt_ref, fzg_ref, carry, ubuf, a_s, b_s, hc):
    i = pl.program_id(0)
    rows = x_ref.shape[0]
    x = x_ref[...]
    ms = jnp.mean(x * x, axis=-1, keepdims=True)
    h = (x * lax.rsqrt(ms + NORM_EPS) * g_ref[...]).astype(jnp.bfloat16)
    proj = lambda name: _dot(h, w_ref[:, seg[name][0]:seg[name][1]])
    bd = bd_ref[...]
    bf = jnp.bfloat16

    @pl.when(i % tiles_per_seq == 0)
    def _():
        carry[...] = jnp.zeros_like(carry)
        ubuf[0:SUBLANES, :] = jnp.zeros((SUBLANES, ubuf.shape[1]), jnp.float32)
        hc[...] = jnp.zeros_like(hc)

    uz = proj("lru")
    lru_w = uz.shape[1] // 2
    ylru_ref[...] = _rglru_tile(uz[:, 0:lru_w], uz[:, lru_w:], cw_ref, cb_ref, wg_ref, bg_ref, lam_ref,
                                ubuf, a_s, b_s, hc).astype(bf)
    kc_ref[...] = proj("kc")
    vc_ref[...] = proj("vc")
    nqn_ref[...] = _heads_rmsnorm(proj("nq"), bd, gain_ref[0:1, :]).astype(bf)
    nzg_ref[...] = _silu(proj("nz")).astype(bf)
    fqn_ref[...] = _heads_rmsnorm(proj("fq"), bd, gain_ref[2:3, :]).astype(bf)
    fkn_ref[...] = _heads_rmsnorm(proj("fk"), bd, gain_ref[3:4, :]).astype(bf)
    fzg_ref[...] = _silu(proj("fz")).astype(bf)
    pad_row = lax.broadcasted_iota(jnp.int32, (VROWS - HEAD_DIM, rows), 0)
    ones_pad = jnp.where(pad_row == 0, 1.0, 0.0).astype(bf)

    def store_vt(ref, slot, v_t):
        ref[0, slot * VROWS:slot * VROWS + HEAD_DIM, :] = v_t.astype(bf)
        ref[0, slot * VROWS + HEAD_DIM:(slot + 1) * VROWS, :] = ones_pad

    fv = proj("fv")
    for j in range(fv.shape[1] // LANES):
        fv_t = fv[:, j * LANES:(j + 1) * LANES].T
        store_vt(fvt_ref, 2 * j, fv_t[0:HEAD_DIM, :])
        store_vt(fvt_ref, 2 * j + 1, fv_t[HEAD_DIM:2 * HEAD_DIM, :])

    nkv = proj("nkv")
    nkv_n = _heads_rmsnorm(nkv, bd, gain_ref[1:2, :])
    lane = lax.broadcasted_iota(jnp.int32, (rows, LANES), 1)
    pos = (i % tiles_per_seq) * rows + lax.broadcasted_iota(jnp.int32, (rows, LANES), 0)
    onehot = jnp.where(lane - HEAD_DIM == pos // SLC_LEN, 1.0, 0.0)
    for g in range(NSA_KV_HEADS):
        b0 = 2 * g * LANES
        nks_ref[:, g * LANES:(g + 1) * LANES] = jnp.where(lane < HEAD_DIM, nkv_n[:, b0:b0 + LANES], onehot).astype(bf)
        nkw_ref[:, g * LANES:(g + 1) * LANES] = jnp.where(lane < HEAD_DIM, nkv_n[:, b0 + LANES:b0 + 2 * LANES],
                                                          0.0).astype(bf)
    for j in range(nkv.shape[1] // LANES):
        store_vt(nvt_ref, j, nkv[:, j * LANES:(j + 1) * LANES].T[HEAD_DIM:2 * HEAD_DIM, :])

    misc = proj("misc") + mb_ref[...]
    ngt_ref[0] = _sigmoid(misc).T
    log_f = jnp.minimum(misc, 0.0) - jnp.log1p(jnp.exp(-jnp.abs(misc)))

    tri =(lax.broadcasted_iota(jnp.int32, (rows, rows), 0)
           >= lax.broadcasted_iota(jnp.int32, (rows, rows), 1)).astype(bf)
    cum = _dot01_left(tri, log_f) + carry[...]
    carry[...] = cum[rows - 1:rows, :]
    terms = _split3(cum * (-LOG2E))
    fex = _dot(terms[0], place_ref[0])
    for t in range(1, BIAS_TERMS):
        fex = fex + _dot(terms[t], place_ref[t])
    fex_ref[...] = fex.astype(bf)


def _in_proj(x2d, norm_g, w_perm, seg, seq, ones_bd, gain_rows, misc_bias, place, lru_params):
    n, d = x2d.shape
    aw = seg["nq"][1] - seg["nq"][0]
    lru_w = (seg["lru"][1] - seg["lru"][0]) // 2
    rows = PROJ_ROWS
    assert n % rows == 0 and seq % rows == 0
    n_tiles = n // rows
    f32, bf = jnp.float32, jnp.bfloat16
    row_out = lambda w, dt: (jax.ShapeDtypeStruct((n, w), dt), pl.BlockSpec((rows, w), lambda i: (i, 0)))
    t_out = lambda w, dt: (jax.ShapeDtypeStruct((n_tiles, w, rows), dt), pl.BlockSpec((1, w, rows), lambda i: (i, 0, 0)))
    outs = [row_out(lru_w, bf),
            row_out(aw, bf),
            row_out(LANES, f32), row_out(LANES, f32),
            row_out(NSA_KV_HEADS * LANES, bf),
            row_out(NSA_KV_HEADS * LANES, bf),
            t_out(2 * NSA_KV_HEADS * VROWS, bf),
            t_out(LANES, f32),
            row_out(aw, bf),
            row_out(aw, bf), row_out(aw, bf), row_out(aw, bf),
            t_out(aw // HEAD_DIM * VROWS, bf),
            row_out(aw, bf)]
    const2 = lambda a: pl.BlockSpec(a.shape, lambda i: (0, 0))
    return pl.pallas_call(
        functools.partial(_in_proj_kernel, seg, seq // rows),
        out_shape=[o[0] for o in outs],
        grid=(n_tiles,),
        in_specs=[pl.BlockSpec((rows, d), lambda i: (i, 0)),
                  pl.BlockSpec((1, d), lambda i: (0, 0)),
                  const2(w_perm), const2(ones_bd), const2(gain_rows), const2(misc_bias),
                  pl.BlockSpec(place.shape, lambda i: (0, 0, 0))] + [const2(a) for a in lru_params],
        out_specs=[o[1] for o in outs],
        scratch_shapes=[pltpu.VMEM((1, LANES), jnp.float32),
                        pltpu.VMEM((rows + SUBLANES, lru_w), jnp.float32),
                        pltpu.VMEM((rows, lru_w), jnp.float32),
                        pltpu.VMEM((rows, lru_w), jnp.float32),
                        pltpu.VMEM((1, lru_w), jnp.float32)],
        compiler_params=_params(("arbitrary",)),
        name="in_proj",
    )(x2d, norm_g.reshape(1, d), w_perm, ones_bd, gain_rows, misc_bias, place, *lru_params)


def _out_proj_kernel(x_ref, ya_ref, yb_ref, yc_ref, w_ref, o_ref):
    wa = ya_ref.shape[1]
    wb = yb_ref.shape[1]
    acc = x_ref[...]
    acc = acc + _dot(ya_ref[...], w_ref[0:wa, :])
    acc = acc + _dot(yb_ref[...], w_ref[wa:wa + wb, :])
    acc = acc + _dot(yc_ref[...], w_ref[wa + wb:, :])
    o_ref[...] = acc


def _out_proj(x2d, ya, yb, yc, w_out_bf16):
    n, d = x2d.shape
    row = lambda i: (i, 0)
    return pl.pallas_call(
        _out_proj_kernel,
        out_shape=jax.ShapeDtypeStruct((n, d), jnp.float32),
        grid=(n // OUT_ROWS,),
        in_specs=[pl.BlockSpec((OUT_ROWS, d), row),
                  pl.BlockSpec((OUT_ROWS, ya.shape[1]), row),
                  pl.BlockSpec((OUT_ROWS, yb.shape[1]), row),
                  pl.BlockSpec((OUT_ROWS, yc.shape[1]), row),
                  pl.BlockSpec(w_out_bf16.shape, lambda i: (0, 0))],
        out_specs=pl.BlockSpec((OUT_ROWS, d), row),
        compiler_params=_params(("parallel",)),
        name="out_proj",
    )(x2d, ya, yb, yc, w_out_bf16)


def _rglru_tile(u, z, cw_ref, cb_ref, wg_ref, bg_ref, lam_ref, ubuf, a_s, b_s, hc):
    rows = a_s.shape[0]
    width = a_s.shape[1]
    ubuf[SUBLANES:SUBLANES + rows, :] = u
    xc = cb_ref[...] + cw_ref[CONV_WIDTH - 1:CONV_WIDTH, :] * u
    for k in range(CONV_WIDTH - 1):
        shift = CONV_WIDTH - 1 - k
        xc = xc + cw_ref[k:k + 1, :] * ubuf[SUBLANES - shift:SUBLANES - shift + rows, :]
    ubuf[0:SUBLANES, :] = u[rows - SUBLANES:rows, :]

    gates = _dot(xc.astype(jnp.bfloat16), wg_ref[...]) + bg_ref[...]
    r = _sigmoid(gates[:, 0:width])
    ig = _sigmoid(gates[:, width:2 * width])
    nlam = -lam_ref[...]
    softplus = jnp.maximum(nlam, 0.0) + jnp.log1p(jnp.exp(-jnp.abs(nlam)))
    log_a = (-LRU_C) * r * softplus
    a = jnp.exp(log_a)
    a_s[...] = a
    b_s[...] = jnp.sqrt(1.0 - a * a) * (ig * xc)

    row = lax.broadcasted_iota(jnp.int32, (SUBLANES, width), 0)

    h_prev = hc[...]
    for gi in range(rows // SUBLANES):
        r0 = gi * SUBLANES
        a8 = a_s[r0:r0 + SUBLANES, :]
        b8 = b_s[r0:r0 + SUBLANES, :]
        d = 1
        while d < SUBLANES:
            a_sh = jnp.where(row >= d, pltpu.roll(a8, d, 0), 1.0)
            b_sh = jnp.where(row >= d, pltpu.roll(b8, d, 0), 0.0)
            b8 = b8 + a8 * b_sh
            a8 = a8 * a_sh
            d *= 2
        h8 = b8 + a8 * h_prev
        b_s[r0:r0 + SUBLANES, :] = h8
        h_prev = h8[SUBLANES - 1:SUBLANES, :]
    hc[...] = h_prev
    return b_s[...] * _silu(z)


def _cmp_kernel(q_ref, kc_ref, vc_ref, kg_ref, pek_ref, pev_ref, wk_ref, wv_ref, qaug_ref, ocmp_ref):
    seq = q_ref.shape[1]
    n_blk = seq // CMP_STRIDE
    n_cmp = (seq - CMP_LEN) // CMP_STRIDE + 1
    n_slc = seq // SLC_LEN
    half = CMP_LEN // CMP_STRIDE
    assert half == 2 and n_blk == LANES
    kvw = NSA_KV_HEADS * HEAD_DIM
    heads_per_blk = LANES // HEAD_DIM

    acc = [jnp.zeros((n_blk, kvw), jnp.float32) for _ in range(4)]
    for l in range(CMP_STRIDE):
        xk = kc_ref[0, pl.ds(l, n_blk, stride=CMP_STRIDE), :]
        xv = vc_ref[0, pl.ds(l, n_blk, stride=CMP_STRIDE), :]
        l2 = CMP_STRIDE + l
        acc[0] = acc[0] + _dot((xk + pek_ref[l:l + 1, :]).astype(jnp.bfloat16), wk_ref[l])
        acc[1] = acc[1] + _dot((xk + pek_ref[l2:l2 + 1, :]).astype(jnp.bfloat16), wk_ref[l2])
        acc[2] = acc[2] + _dot((xv + pev_ref[l:l + 1, :]).astype(jnp.bfloat16), wv_ref[l])
        acc[3] = acc[3] + _dot((xv + pev_ref[l2:l2 + 1, :]).astype(jnp.bfloat16), wv_ref[l2])
    k_cmp = acc[0] + pltpu.roll(acc[1], n_blk - 1, 0)
    v_cmp = acc[2] + pltpu.roll(acc[3], n_blk - 1, 0)
    v_cmp_t = v_cmp.T.astype(jnp.bfloat16)
    k_n = [_head_rmsnorm(k_cmp[:, g * HEAD_DIM:(g + 1) * HEAD_DIM], kg_ref[...]).astype(jnp.bfloat16)
           for g in range(NSA_KV_HEADS)]
    v_t = [v_cmp_t[g * HEAD_DIM:(g + 1) * HEAD_DIM, :] for g in range(NSA_KV_HEADS)]

    n_idx = lax.broadcasted_iota(jnp.int32, (n_blk, TQ), 0)
    j_idx = lax.broadcasted_iota(jnp.int32, (n_slc, TQ), 0)
    oj = lax.broadcasted_iota(jnp.int32, (n_slc, n_blk), 0) * SLC_LEN
    on = lax.broadcasted_iota(jnp.int32, (n_slc, n_blk), 1) * CMP_STRIDE
    overlap_t = ((on <= oj + SLC_LEN - 1) & (on + CMP_LEN - 1 >= oj)
                 & (on < n_cmp * CMP_STRIDE)).astype(jnp.bfloat16)

    def q_tile(qi, carry):
        t0 = pl.multiple_of(qi * TQ, TQ)
        t_row = t0 + lax.broadcasted_iota(jnp.int32, (1, TQ), 1)
        valid_cmp = (n_idx * CMP_STRIDE + CMP_LEN - 1 <= t_row) & (n_idx < n_cmp)
        blk_t = t_row // SLC_LEN
        valid_slc = j_idx <= blk_t
        forced = (j_idx == 0) | (j_idx == blk_t) | (j_idx == blk_t - 1)
        for g in range(NSA_KV_HEADS):
            q_heads = []
            for r in range(NSA_GROUP):
                h = g * NSA_GROUP + r
                blk = h // heads_per_blk
                pair = q_ref[0, pl.ds(t0, TQ), blk * LANES:(blk + 1) * LANES].astype(jnp.float32)
                j = h % heads_per_blk
                q_heads.append(pair[:, j * HEAD_DIM:(j + 1) * HEAD_DIM].astype(jnp.bfloat16))
            ss = [jnp.where(valid_cmp, _dot_nt(k_n[g], qh), NEG_INF) for qh in q_heads]
            es = [jnp.where(valid_cmp, jnp.exp2(s - jnp.max(s, axis=0, keepdims=True)), 0.0)
                  for s in ss]
            ps = [e / jnp.maximum(jnp.sum(e, axis=0, keepdims=True), 1e-30) for e in es]
            for r in range(NSA_GROUP):
                h = g * NSA_GROUP + r
                ocmp_ref[0, qi, h * HEAD_DIM:(h + 1) * HEAD_DIM, :] = _dot(v_t[g], ps[r].astype(jnp.bfloat16))
            p_sum = ps[0] + ps[1] + ps[2] + ps[3]
            imp = _dot01_left(overlap_t, p_sum)
            score = jnp.where(valid_slc, jnp.where(forced, FORCE_SCORE, imp), NEG_INF)
            rank = jnp.zeros((n_slc, TQ), jnp.float32)
            for jp in range(n_slc):
                other = score[jp:jp + 1, :]
                ahead = (other > score) | ((other == score) & (j_idx > jp))
                rank = rank + jnp.where(ahead, 1.0, 0.0)
            bias_t = jnp.where((rank < N_SELECT) & valid_slc, 0.0, NEG_INF)
            bias_t = jnp.concatenate([bias_t, jnp.zeros((LANES - n_slc, TQ), jnp.float32)], axis=0)
            bias = bias_t.T[:, 0:HEAD_DIM].astype(jnp.bfloat16)
            for r in range(NSA_GROUP):
                qaug_ref[0, g * NSA_GROUP + r, pl.ds(t0, TQ), :] = jnp.concatenate([q_heads[r], bias], axis=-1)
        return carry

    lax.fori_loop(0, seq // TQ, q_tile, 0)


def _nsa_cmp(nqn, kc, vc, k_g0, pe_k2, pe_v2, wk_bd, wv_bd):
    bsz, seq, qw = nqn.shape
    n_heads = qw // HEAD_DIM
    nq_t = seq // TQ
    assert seq // SLC_LEN <= HEAD_DIM
    c2 = lambda b: (0, 0)
    c3 = lambda b: (0, 0, 0)
    return pl.pallas_call(
        _cmp_kernel,
        out_shape=[jax.ShapeDtypeStruct((bsz, n_heads, seq, LANES), jnp.bfloat16),
                   jax.ShapeDtypeStruct((bsz, nq_t, qw, TQ), jnp.float32)],
        grid=(bsz,),
        in_specs=[pl.BlockSpec((1, seq, qw), lambda b: (b, 0, 0)),
                  pl.BlockSpec((1, seq, kc.shape[2]), lambda b: (b, 0, 0)),
                  pl.BlockSpec((1, seq, vc.shape[2]), lambda b: (b, 0, 0)),
                  pl.BlockSpec((1, HEAD_DIM), c2),
                  pl.BlockSpec(pe_k2.shape, c2),
                  pl.BlockSpec(pe_v2.shape, c2),
                  pl.BlockSpec(wk_bd.shape, c3),
                  pl.BlockSpec(wv_bd.shape, c3)],
        out_specs=[pl.BlockSpec((1, n_heads, seq, LANES), lambda b: (b, 0, 0, 0)),
                   pl.BlockSpec((1, nq_t, qw, TQ), lambda b: (b, 0, 0, 0))],
        compiler_params=_params(("parallel",)),
        name="nsa_compress",
    )(nqn, kc, vc, k_g0.reshape(1, HEAD_DIM), pe_k2, pe_v2, wk_bd, wv_bd)


ITEM_FIELDS = 5
def _item_table(n_q_tiles, mask_specs, window=None):
    def mask_id(spec):
        if spec not in mask_specs:
            mask_specs.append(spec)
        return mask_specs.index(spec)

    items = []
    for qi in range(n_q_tiles):
        t0 = qi * TQ
        first_chunk = 0 if window is None else max(t0 - window + 1, 0) // KC
        chunks = list(range(first_chunk, (t0 + TQ - 1) // KC + 1))
        for n, c in enumerate(chunks):
            needs_causal = c * KC + KC - 1 > t0
            needs_edge = window is not None and c * KC <= t0 + TQ - 1 - window
            assert not (needs_causal and needs_edge)
            spec = ("causal", t0 - c * KC) if needs_causal else (
                ("edge", t0 - c * KC - window) if needs_edge else ("none", 0))
            items.append((qi, c, int(n == 0), int(n == len(chunks) - 1), mask_id(spec)))
    if len(items) % 2:
        items.append((items[-1][0], items[-1][1], 0, 0, mask_id(("all", 0))))
    return np.asarray(items, np.int32).T


def _fill_masks(mask_s, mask_specs):
    @pl.when((pl.program_id(0) == 0) & (pl.program_id(1) == 0))
    def _():
        n = mask_s.shape[1]
        r = lax.broadcasted_iota(jnp.int32, (n, TQ), 0)
        q = lax.broadcasted_iota(jnp.int32, (n, TQ), 1)
        for i, (kind, off) in enumerate(mask_specs):
            keep = {"none": r >= 0, "all": r < 0, "causal": r <= q + off, "edge": r > q + off}[kind]
            mask_s[i] = jnp.where(keep, 0.0, NEG_INF)


def _flash_pipeline(tbl_ref, row0, n_items, n_keys, n_heads, qs_of, k_of, vt_of, s_buf, mask_s, on_last):
    heads = range(n_heads)

    def produce(t, slot, h, qs, bias):
        s = _dot_nt(k_of(h, tbl_ref[row0 + 1, t]), qs[h]) + bias
        s_buf[slot, h, 0:n_keys, :] = s
        return jnp.max(s, axis=0, keepdims=True)

    def item(t, slot, carry):
        states, col_max = carry
        qi = tbl_ref[row0, t]
        chunk = tbl_ref[row0 + 1, t]
        first = tbl_ref[row0 + 2, t] == 1
        last = tbl_ref[row0 + 3, t]
        t_next = jnp.minimum(t + 1, n_items - 1)
        qs_next = qs_of(tbl_ref[row0, t_next])
        bias_next = mask_s[tbl_ref[row0 + 4, t_next]]
        out, col_max_next = [], []
        for h in heads:
            col_max_next.append(produce(t_next, 1 - slot, h, qs_next, bias_next))
            m_prev = jnp.where(first, NEG_INF, states[h][0])
            m_new = jnp.maximum(m_prev, col_max[h])
            p = jnp.exp2(s_buf[slot, h, 0:n_keys, :] - m_new).astype(jnp.bfloat16)
            acc = jnp.exp2(m_prev - m_new) * jnp.where(first, 0.0, states[h][1])
            for j, v_t in enumerate(vt_of(h, chunk)):
                acc = acc + _dot(v_t, p[j * TQ:(j + 1) * TQ, :])
            out.append((m_new, acc))

        @pl.when(last == 1)
        def _():
            on_last(qi, [o[1] for o in out])

        return tuple(out), tuple(col_max_next)

    qs0 = qs_of(tbl_ref[row0, 0])
    bias0 = mask_s[tbl_ref[row0 + 4, 0]]
    col_max0 = tuple(produce(0, 0, h, qs0, bias0) for h in heads)
    init = tuple((jnp.full((1, TQ), NEG_INF, jnp.float32), jnp.zeros((VROWS, TQ), jnp.float32)) for _ in heads)
    lax.fori_loop(0, n_items // 2, lambda i, c: item(2 * i + 1, 1, item(2 * i, 0, c)), (init, col_max0))


def _normalise(acc):
    return acc[0:HEAD_DIM, :] / jnp.maximum(acc[HEAD_DIM:HEAD_DIM + 1, :], 1e-30)


def _store_gated(y_ref, zg_ref, outs, t0):
    for pair in range(len(outs) // 2):
        o2 = jnp.concatenate(outs[2 * pair:2 * pair + 2], axis=0).T
        lanes = slice(pair * LANES, (pair + 1) * LANES)
        zg = zg_ref[0, pl.ds(t0, TQ), lanes].astype(jnp.float32)
        y_ref[0, pl.ds(t0, TQ), lanes] = (o2 * zg).astype(y_ref.dtype)


def _nsa_kernel(n_slc_items, n_win_items, mask_specs, tbl_ref, qaug_ref, ks_ref, kw_ref, vt_ref, gt_ref, zg_ref,
                ocmp_ref, y_ref, s_buf, mask_s, oslc_s, owin_s):
    g_id = pl.program_id(1)
    g_row0 = pl.multiple_of(g_id * GATE_GROUP_STRIDE, GATE_GROUP_STRIDE)
    tiles_per_chunk = KC // TQ
    _fill_masks(mask_s, mask_specs)

    def qs_of(qi):
        t0 = pl.multiple_of(qi * TQ, TQ)
        return [qaug_ref[0, r, pl.ds(t0, TQ), :] for r in range(NSA_GROUP)]

    def branch(row0, n_items, k_ref, v_rows, out_s):
        def done(qi, accs):
            for r in range(NSA_GROUP):
                out_s[qi, r] = _normalise(accs[r])

        _flash_pipeline(
            tbl_ref, row0, n_items, KC, NSA_GROUP, qs_of,
            lambda h, c: k_ref[0, pl.ds(pl.multiple_of(c * KC, KC), KC), :],
            lambda h, c: [vt_ref[0, c * tiles_per_chunk + j, v_rows, :] for j in range(tiles_per_chunk)],
            s_buf, mask_s, done)

    branch(0, n_slc_items, ks_ref, slice(0, VROWS), oslc_s)
    branch(ITEM_FIELDS, n_win_items, kw_ref, slice(VROWS, 2 * VROWS), owin_s)

    def combine(qi, carry):
        gates = gt_ref[0, qi, pl.ds(g_row0, GATE_GROUP_STRIDE), :]
        outs = []
        for r in range(NSA_GROUP):
            o_cmp = ocmp_ref[0, qi, r * HEAD_DIM:(r + 1) * HEAD_DIM, :]
            outs.append(gates[3 * r:3 * r + 1, :] * o_cmp + gates[3 * r + 1:3 * r + 2, :] * oslc_s[qi, r]
                        + gates[3 * r + 2:3 * r + 3, :] * owin_s[qi, r])
        _store_gated(y_ref, zg_ref, outs, pl.multiple_of(qi * TQ, TQ))
        return carry

    lax.fori_loop(0, oslc_s.shape[0], combine, 0)


def _nsa_attend(qaug, nks, nkw, nvt, ngt, nzg, ocmp_t):
    bsz, n_heads, seq, _ = qaug.shape
    gw = NSA_GROUP * HEAD_DIM
    nq_t = seq // TQ
    assert seq % KC == 0 and WINDOW % TQ == 0
    mask_specs = []
    slc_items = _item_table(nq_t, mask_specs)
    win_items = _item_table(nq_t, mask_specs, WINDOW)
    width = max(slc_items.shape[1], win_items.shape[1])
    table = np.zeros((2 * ITEM_FIELDS, width), np.int32)
    table[0:ITEM_FIELDS, :slc_items.shape[1]] = slc_items
    table[ITEM_FIELDS:, :win_items.shape[1]] = win_items
    grid_spec = pltpu.PrefetchScalarGridSpec(
        num_scalar_prefetch=1,
        grid=(bsz, NSA_KV_HEADS),
        in_specs=[pl.BlockSpec((1, NSA_GROUP, seq, LANES), lambda b, g, t: (b, g, 0, 0)),
                  pl.BlockSpec((1, seq, LANES), lambda b, g, t: (b, 0, g)),
                  pl.BlockSpec((1, seq, LANES), lambda b, g, t: (b, 0, g)),
                  pl.BlockSpec((1, nq_t, 2 * VROWS, TQ), lambda b, g, t: (b, 0, g, 0)),
                  pl.BlockSpec((1, nq_t, LANES, TQ), lambda b, g, t: (b, 0, 0, 0)),
                  pl.BlockSpec((1, seq, gw), lambda b, g, t: (b, 0, g)),
                  pl.BlockSpec((1, nq_t, gw, TQ), lambda b, g, t: (b, 0, g, 0))],
        out_specs=pl.BlockSpec((1, seq, gw), lambda b, g, t: (b, 0, g)),
        scratch_shapes=[pltpu.VMEM((2, NSA_GROUP, KC, TQ), jnp.float32),
                        pltpu.VMEM((len(mask_specs), KC, TQ), jnp.float32),
                        pltpu.VMEM((nq_t, NSA_GROUP, HEAD_DIM, TQ), jnp.float32),
                        pltpu.VMEM((nq_t, NSA_GROUP, HEAD_DIM, TQ), jnp.float32)])
    return pl.pallas_call(
        functools.partial(_nsa_kernel, slc_items.shape[1], win_items.shape[1], tuple(mask_specs)),
        out_shape=jax.ShapeDtypeStruct((bsz, seq, n_heads * HEAD_DIM), jnp.bfloat16),
        grid_spec=grid_spec,
        compiler_params=_params(("arbitrary", "arbitrary")),
        name="nsa_attend",
    )(jnp.asarray(table), qaug, nks, nkw, nvt, ngt, nzg, ocmp_t)


def _fox_kernel(n_items, mask_specs, tbl_ref, q_ref, k_ref, ex_ref, vt_ref, zg_ref, y_ref, s_buf, mask_s, o_s):
    n_heads = q_ref.shape[2] // HEAD_DIM
    heads_per_blk = LANES // HEAD_DIM
    tiles_per_chunk = KC // TQ
    _fill_masks(mask_s, mask_specs)
    lane = lax.broadcasted_iota(jnp.int32, (TQ, LANES), 1)
    own = [(lane >= j * HEAD_DIM) & (lane < (j + 1) * HEAD_DIM) for j in range(heads_per_blk)]
    ones = [jnp.where((lane >= j * BIAS_TERMS) & (lane < (j + 1) * BIAS_TERMS), 1.0, 0.0).astype(jnp.bfloat16)
            for j in range(heads_per_blk)]

    def qs_of(qi):
        t0 = pl.multiple_of(qi * TQ, TQ)
        qs = []
        for blk in range(n_heads // heads_per_blk):
            pair = q_ref[0, pl.ds(t0, TQ), blk * LANES:(blk + 1) * LANES]
            for j in range(heads_per_blk):
                qs.append(jnp.concatenate([jnp.where(own[j], pair, jnp.zeros_like(pair)), ones[j]], axis=-1))
        return qs

    def k_of(h, c):
        k0 = pl.multiple_of(c * KC, KC)
        lanes = slice((h // heads_per_blk) * LANES, (h // heads_per_blk + 1) * LANES)
        return jnp.concatenate([k_ref[0, pl.ds(k0, KC), lanes], ex_ref[0, pl.ds(k0, KC), lanes]], axis=-1)

    def done(qi, accs):
        for h in range(n_heads):
            o_s[qi, h] = _normalise(accs[h])

    _flash_pipeline(
        tbl_ref, 0, n_items, KC, n_heads, qs_of, k_of,
        lambda h, c: [vt_ref[0, c * tiles_per_chunk + j, h * VROWS:(h + 1) * VROWS, :] for j in range(tiles_per_chunk)],
        s_buf, mask_s, done)

    def finish(qi, carry):
        _store_gated(y_ref, zg_ref, [o_s[qi, h] for h in range(n_heads)], pl.multiple_of(qi * TQ, TQ))
        return carry

    lax.fori_loop(0, o_s.shape[0], finish, 0)


def _fox_attend(fqn, fkn, fex, fvt, fzg):
    bsz, seq, width = fqn.shape
    bw = FOX_HEADS_PER_STEP * HEAD_DIM
    assert seq % KC == 0 and width % bw == 0
    mask_specs = []
    table = _item_table(seq // TQ, mask_specs)
    blk = pl.BlockSpec((1, seq, bw), lambda b, p, t: (b, 0, p))
    grid_spec = pltpu.PrefetchScalarGridSpec(
        num_scalar_prefetch=1,
        grid=(bsz, width // bw),
        in_specs=[blk, blk, blk,
                  pl.BlockSpec((1, seq // TQ, FOX_HEADS_PER_STEP * VROWS, TQ), lambda b, p, t: (b, 0, p, 0)),
                  blk],
        out_specs=blk,
        scratch_shapes=[pltpu.VMEM((2, FOX_HEADS_PER_STEP, KC, TQ), jnp.float32),
                        pltpu.VMEM((len(mask_specs), KC, TQ), jnp.float32),
                        pltpu.VMEM((seq // TQ, FOX_HEADS_PER_STEP, HEAD_DIM, TQ), jnp.float32)])
    return pl.pallas_call(
        functools.partial(_fox_kernel, table.shape[1], tuple(mask_specs)),
        out_shape=jax.ShapeDtypeStruct((bsz, seq, width), jnp.bfloat16),
        grid_spec=grid_spec,
        compiler_params=_params(("arbitrary", "arbitrary")),
        name="fox_attend",
    )(jnp.asarray(table), fqn, fkn, fex, fvt, fzg)


def _block_diag(blocks):
    n, r, c = blocks.shape
    eye = jnp.eye(n, dtype=blocks.dtype)
    return (eye[:, None, :, None] * blocks[:, :, None, :]).reshape(n * r, n * c)


def _layer_layout(d_model):
    lru_w = d_model // 2
    n_heads = d_model // 128
    aw = n_heads * HEAD_DIM
    kvw = NSA_KV_HEADS * HEAD_DIM
    splits = (lru_w, lru_w, aw, kvw, kvw, kvw, kvw, kvw, kvw, 3 * n_heads, aw, aw, aw, aw, n_heads, aw)
    offs = np.concatenate([[0], np.cumsum(splits)])
    names = ("lru_u", "lru_z", "nq", "kc", "vc", "ks", "vs", "kw", "vw", "gl", "nz", "fq", "fk", "fv", "fl", "fz")
    col = {n: np.arange(offs[i], offs[i + 1]) for i, n in enumerate(names)}
    nkv = []
    for g in range(NSA_KV_HEADS):
        for n in ("ks", "vs", "kw", "vw"):
            nkv.append(col[n][g * HEAD_DIM:(g + 1) * HEAD_DIM])
    misc = np.full((LANES,), -1, np.int64)
    per_group = 3 * NSA_GROUP
    for g in range(NSA_KV_HEADS):
        misc[g * GATE_GROUP_STRIDE:g * GATE_GROUP_STRIDE + per_group] = col["gl"][g * per_group:(g + 1) * per_group]
    misc[FORGET_LANE0:FORGET_LANE0 + n_heads] = col["fl"]
    segs = [("lru", np.concatenate([col["lru_u"], col["lru_z"]])),
            ("nq", col["nq"]),
            ("kc", col["kc"]), ("vc", col["vc"]),
            ("nkv", np.concatenate(nkv)),
            ("nz", col["nz"]),
            ("fq", col["fq"]), ("fk", col["fk"]), ("fv", col["fv"]), ("fz", col["fz"]),
            ("misc", misc)]
    return segs, n_heads


def _permute_w_in(w_in, segs):
    d = w_in.shape[0]
    idx = np.concatenate([c for _, c in segs])
    pieces, start = [], 0
    for end in range(1, len(idx) + 1):
        run_continues = end < len(idx) and ((idx[end] < 0 and idx[start] < 0)
                                            or (idx[start] >= 0 and idx[end] == idx[end - 1] + 1))
        if not run_continues:
            if idx[start] < 0:
                pieces.append(jnp.zeros((d, end - start), jnp.bfloat16))
            else:
                pieces.append(w_in[:, int(idx[start]):int(idx[start]) + end - start].astype(jnp.bfloat16))
            start = end
    return jnp.concatenate(pieces, axis=1)


def _hybrid_layer(x, norm_g, w_in, w_out, conv_w, conv_b, lru_wa, lru_ba, lru_wx, lru_bx, lru_lambda,
                  nsa_q_g, nsa_k_g, cmp_pe_k, cmp_pe_v, cmp_wk, cmp_wv, nsa_gate_b,
                  fox_q_g, fox_k_g, fox_f_b):
    bsz, seq, d_model = x.shape
    segs, n_heads = _layer_layout(d_model)
    aw = n_heads * HEAD_DIM
    heads_per_blk = LANES // HEAD_DIM
    assert seq % TQ == 0 and seq // CMP_STRIDE == LANES and n_heads * 3 <= 2 * GATE_GROUP_STRIDE
    assert NSA_KV_HEADS * 4 * HEAD_DIM == aw and BIAS_TERMS * heads_per_blk <= LANES
    seg, off = {}, 0
    for name, c in segs:
        seg[name] = (off, off + len(c))
        off += len(c)
    x2d = x.reshape(bsz * seq, d_model)

    ones_bd = _block_diag(jnp.ones((n_heads, HEAD_DIM, HEAD_DIM), jnp.bfloat16))
    one = jnp.ones((HEAD_DIM,), jnp.float32)
    nkv_gain = jnp.concatenate([nsa_k_g[1], one, nsa_k_g[2], one] * NSA_KV_HEADS)
    q_scale = ATTN_SCALE * LOG2E
    gain_rows = jnp.stack([jnp.tile(nsa_q_g, n_heads) * q_scale, nkv_gain,
                           jnp.tile(fox_q_g, n_heads) * q_scale, jnp.tile(fox_k_g, n_heads)])
    per_group = 3 * NSA_GROUP
    misc_bias = jnp.zeros((1, LANES), jnp.float32)
    for g in range(NSA_KV_HEADS):
        misc_bias = misc_bias.at[0, g * GATE_GROUP_STRIDE:g * GATE_GROUP_STRIDE + per_group].set(
            nsa_gate_b[g * per_group:(g + 1) * per_group])
    misc_bias = misc_bias.at[0, FORGET_LANE0:FORGET_LANE0 + n_heads].set(fox_f_b)
    place = np.zeros((BIAS_TERMS, LANES, aw), np.float32)
    for h in range(n_heads):
        for t in range(BIAS_TERMS):
            place[t, FORGET_LANE0 + h, (h // heads_per_blk) * LANES + (h % heads_per_blk) * BIAS_TERMS + t] = 1.0

    lru_w = d_model // 2
    w_gates = jnp.concatenate([_block_diag(lru_wa), _block_diag(lru_wx)], axis=1).astype(jnp.bfloat16)
    lru_params = (conv_w, conv_b.reshape(1, lru_w), w_gates, jnp.concatenate([lru_ba, lru_bx]).reshape(1, 2 * lru_w),
                  lru_lambda.reshape(1, lru_w))

    (y_lru, nqn, kc, vc, nks, nkw, nvt, ngt, nzg, fqn, fkn, fex, fvt, fzg) = _in_proj(
        x2d, norm_g, _permute_w_in(w_in, segs), seg, seq, ones_bd, gain_rows, misc_bias,
        jnp.asarray(place, jnp.bfloat16), lru_params)
    tok = lambda a: a.reshape(bsz, seq, a.shape[1])
    tiled = lambda a: a.reshape(bsz, seq // TQ, a.shape[1], TQ)

    tile2 = lambda a: jnp.tile(a, (1, NSA_KV_HEADS))
    bd2 = lambda w: jax.vmap(lambda m: _block_diag(jnp.stack([m] * NSA_KV_HEADS)))(w).astype(jnp.bfloat16)
    qaug, ocmp_t = _nsa_cmp(tok(nqn), tok(kc), tok(vc), nsa_k_g[0], tile2(cmp_pe_k), tile2(cmp_pe_v),
                            bd2(cmp_wk), bd2(cmp_wv))
    y_nsa = _nsa_attend(qaug, tok(nks), tok(nkw), tiled(nvt), tiled(ngt), tok(nzg), ocmp_t)

    y_fox = _fox_attend(tok(fqn), tok(fkn), tok(fex), tiled(fvt), tok(fzg))

    flat = lambda a: a.reshape(bsz * seq, a.shape[2])
    out = _out_proj(x2d, y_lru, flat(y_nsa), flat(y_fox), w_out.astype(jnp.bfloat16))
    return out.reshape(bsz, seq, d_model)


def kernel(x, norm_g, w_in, w_out, conv_w, conv_b, lru_wa, lru_ba, lru_wx, lru_bx, lru_lambda, nsa_q_g, nsa_k_g, cmp_pe_k, cmp_pe_v, cmp_wk, cmp_wv, nsa_gate_b, fox_q_g, fox_k_g, fox_f_b):
    for l in range(norm_g.shape[0]):
        x = _hybrid_layer(x, norm_g[l], w_in[l], w_out[l], conv_w[l], conv_b[l], lru_wa[l], lru_ba[l],
                          lru_wx[l], lru_bx[l], lru_lambda[l], nsa_q_g[l], nsa_k_g[l], cmp_pe_k[l],
                          cmp_pe_v[l], cmp_wk[l], cmp_wv[l], nsa_gate_b[l], fox_q_g[l], fox_k_g[l],
                          fox_f_b[l])
    return x
```

```python
import functools

import jax
import jax.numpy as jnp
import numpy as np
from jax import lax
from jax.experimental import pallas as pl
from jax.experimental.pallas import tpu as pltpu

HEAD_DIM = 64
LRU_BLOCKS = 8
CONV_WIDTH = 4
LRU_C = 8.0
NSA_KV_HEADS = 2
NSA_GROUP = 4
CMP_LEN = 32
CMP_STRIDE = 16
SLC_LEN = 64
N_SELECT = 16
WINDOW = 512
NORM_EPS = 1e-6
NEG_INF = -1e30
FORCE_SCORE = 1e9
ATTN_SCALE = HEAD_DIM ** -0.5
LOG2E = 1.4426950408889634

LANES = 128
SUBLANES = 8
TQ = 256
TA = 512
KC = 512
PROJ_ROWS = TQ
OUT_ROWS = 512
VMEM_LIMIT = 56 * 1024 * 1024

GATE_GROUP_STRIDE = 16
FORGET_LANE0 = 32
FOX_HEADS_PER_STEP = 4
BIAS_TERMS = 3
VROWS = 80

_NT = (((1,), (1,)), ((), ()))


def _dot(a, b):
    return jnp.dot(a, b, preferred_element_type=jnp.float32)


def _dot_nt(a, b):
    return lax.dot_general(a, b, _NT, preferred_element_type=jnp.float32)


def _split3(x):
    hi = x.astype(jnp.bfloat16)
    r = x - hi.astype(jnp.float32)
    mid = r.astype(jnp.bfloat16)
    lo = (r - mid.astype(jnp.float32)).astype(jnp.bfloat16)
    return hi, mid, lo


def _dot01_left(m01, x):
    hi, mid, lo = _split3(x)
    return _dot(m01, hi) + _dot(m01, mid) + _dot(m01, lo)


def _head_rmsnorm(x, gain):
    ms = jnp.sum(x * x, axis=-1, keepdims=True) * (1.0 / HEAD_DIM)
    return x * lax.rsqrt(ms + NORM_EPS) * gain


def _heads_rmsnorm(x, ones_bd, gain_row):
    ss = _dot((x * x).astype(jnp.bfloat16), ones_bd)
    return x * lax.rsqrt(ss * (1.0 / HEAD_DIM) + NORM_EPS) * gain_row


def _sigmoid(x):
    return 1.0 / (1.0 + jnp.exp(-x))


def _silu(x):
    return x * _sigmoid(x)


def _params(sem):
    return pltpu.CompilerParams(dimension_semantics=sem, vmem_limit_bytes=VMEM_LIMIT)


def _in_proj_kernel(seg, tiles_per_seq, x_ref, g_ref, w_ref, bd_ref, gain_ref, mb_ref, place_ref,
                    cw_ref, cb_ref, wg_ref, bg_ref, lam_ref,
                    ylru_ref, nqn_ref, kc_ref, vc_ref, nks_ref, nkw_ref, nvt_ref, ngt_ref, nzg_ref,
                    fqn_ref, fkn_ref, fex_ref, fvt_ref, fzg_ref, carry, ubuf, a_s, b_s, hc):
    i = pl.program_id(0)
    rows = x_ref.shape[0]
    x = x_ref[...]
    ms = jnp.mean(x * x, axis=-1, keepdims=True)
    h = (x * lax.rsqrt(ms + NORM_EPS) * g_ref[...]).astype(jnp.bfloat16)
    proj = lambda name: _dot(h, w_ref[:, seg[name][0]:seg[name][1]])
    bd = bd_ref[...]
    bf = jnp.bfloat16

    @pl.when(i % tiles_per_seq == 0)
    def _():
        carry[...] = jnp.zeros_like(carry)
        ubuf[0:SUBLANES, :] = jnp.zeros((SUBLANES, ubuf.shape[1]), jnp.float32)
        hc[...] = jnp.zeros_like(hc)

    uz = proj("lru")
    lru_w = uz.shape[1] // 2
    ylru_ref[...] = _rglru_tile(uz[:, 0:lru_w], uz[:, lru_w:], cw_ref, cb_ref, wg_ref, bg_ref, lam_ref,
                                ubuf, a_s, b_s, hc).astype(bf)
    kc_ref[...] = proj("kc")
    vc_ref[...] = proj("vc")
    nqn_ref[...] = _heads_rmsnorm(proj("nq"), bd, gain_ref[0:1, :]).astype(bf)
    nzg_ref[...] = _silu(proj("nz")).astype(bf)
    fqn_ref[...] = _heads_rmsnorm(proj("fq"), bd, gain_ref[2:3, :]).astype(bf)
    fkn_ref[...] = _heads_rmsnorm(proj("fk"), bd, gain_ref[3:4, :]).astype(bf)
    fzg_ref[...] = _silu(proj("fz")).astype(bf)
    pad_row = lax.broadcasted_iota(jnp.int32, (VROWS - HEAD_DIM, rows), 0)
    ones_pad = jnp.where(pad_row == 0, 1.0, 0.0).astype(bf)

    def store_vt(ref, slot, v_t):
        ref[0, slot * VROWS:slot * VROWS + HEAD_DIM, :] = v_t.astype(bf)
        ref[0, slot * VROWS + HEAD_DIM:(slot + 1) * VROWS, :] = ones_pad

    fv = proj("fv")
    for j in range(fv.shape[1] // LANES):
        fv_t = fv[:, j * LANES:(j + 1) * LANES].T
        store_vt(fvt_ref, 2 * j, fv_t[0:HEAD_DIM, :])
        store_vt(fvt_ref, 2 * j + 1, fv_t[HEAD_DIM:2 * HEAD_DIM, :])

    nkv = proj("nkv")
    nkv_n = _heads_rmsnorm(nkv, bd, gain_ref[1:2, :])
    lane = lax.broadcasted_iota(jnp.int32, (rows, LANES), 1)
    pos = (i % tiles_per_seq) * rows + lax.broadcasted_iota(jnp.int32, (rows, LANES), 0)
    onehot = jnp.where(lane - HEAD_DIM == pos // SLC_LEN, 1.0, 0.0)
    for g in range(NSA_KV_HEADS):
        b0 = 2 * g * LANES
        nks_ref[:, g * LANES:(g + 1) * LANES] = jnp.where(lane < HEAD_DIM, nkv_n[:, b0:b0 + LANES], onehot).astype(bf)
        nkw_ref[:, g * LANES:(g + 1) * LANES] = jnp.where(lane < HEAD_DIM, nkv_n[:, b0 + LANES:b0 + 2 * LANES],
                                                          0.0).astype(bf)
    for j in range(nkv.shape[1] // LANES):
        store_vt(nvt_ref, j, nkv[:, j * LANES:(j + 1) * LANES].T[HEAD_DIM:2 * HEAD_DIM, :])

    misc = proj("misc") + mb_ref[...]
    ngt_ref[0] = _sigmoid(misc).T
    log_f = jnp.minimum(misc, 0.0) - jnp.log1p(jnp.exp(-jnp.abs(misc)))

    tri =(lax.broadcasted_iota(jnp.int32, (rows, rows), 0)
           >= lax.broadcasted_iota(jnp.int32, (rows, rows), 1)).astype(bf)
    cum = _dot01_left(tri, log_f) + carry[...]
    carry[...] = cum[rows - 1:rows, :]
    terms = _split3(cum * (-LOG2E))
    fex = _dot(terms[0], place_ref[0])
    for t in range(1, BIAS_TERMS):
        fex = fex + _dot(terms[t], place_ref[t])
    fex_ref[...] = fex.astype(bf)


def _in_proj(x2d, norm_g, w_perm, seg, seq, ones_bd, gain_rows, misc_bias, place, lru_params):
    n, d = x2d.shape
    aw = seg["nq"][1] - seg["nq"][0]
    lru_w = (seg["lru"][1] - seg["lru"][0]) // 2
    rows = PROJ_ROWS
    assert n % rows == 0 and seq % rows == 0
    n_tiles = n // rows
    f32, bf = jnp.float32, jnp.bfloat16
    row_out = lambda w, dt: (jax.ShapeDtypeStruct((n, w), dt), pl.BlockSpec((rows, w), lambda i: (i, 0)))
    t_out = lambda w, dt: (jax.ShapeDtypeStruct((n_tiles, w, rows), dt), pl.BlockSpec((1, w, rows), lambda i: (i, 0, 0)))
    outs = [row_out(lru_w, bf),
            row_out(aw, bf),
            row_out(LANES, f32), row_out(LANES, f32),
            row_out(NSA_KV_HEADS * LANES, bf),
            row_out(NSA_KV_HEADS * LANES, bf),
            t_out(2 * NSA_KV_HEADS * VROWS, bf),
            t_out(LANES, f32),
            row_out(aw, bf),
            row_out(aw, bf), row_out(aw, bf), row_out(aw, bf),
            t_out(aw // HEAD_DIM * VROWS, bf),
            row_out(aw, bf)]
    const2 = lambda a: pl.BlockSpec(a.shape, lambda i: (0, 0))
    return pl.pallas_call(
        functools.partial(_in_proj_kernel, seg, seq // rows),
        out_shape=[o[0] for o in outs],
        grid=(n_tiles,),
        in_specs=[pl.BlockSpec((rows, d), lambda i: (i, 0)),
                  pl.BlockSpec((1, d), lambda i: (0, 0)),
                  const2(w_perm), const2(ones_bd), const2(gain_rows), const2(misc_bias),
                  pl.BlockSpec(place.shape, lambda i: (0, 0, 0))] + [const2(a) for a in lru_params],
        out_specs=[o[1] for o in outs],
        scratch_shapes=[pltpu.VMEM((1, LANES), jnp.float32),
                        pltpu.VMEM((rows + SUBLANES, lru_w), jnp.float32),
                        pltpu.VMEM((rows, lru_w), jnp.float32),
                        pltpu.VMEM((rows, lru_w), jnp.float32),
                        pltpu.VMEM((1, lru_w), jnp.float32)],
        compiler_params=_params(("arbitrary",)),
        name="in_proj",
    )(x2d, norm_g.reshape(1, d), w_perm, ones_bd, gain_rows, misc_bias, place, *lru_params)


def _out_proj_kernel(x_ref, ya_ref, yb_ref, yc_ref, w_ref, o_ref):
    wa = ya_ref.shape[1]
    wb = yb_ref.shape[1]
    acc = x_ref[...]
    acc = acc + _dot(ya_ref[...], w_ref[0:wa, :])
    acc = acc + _dot(yb_ref[...], w_ref[wa:wa + wb, :])
    acc = acc + _dot(yc_ref[...], w_ref[wa + wb:, :])
    o_ref[...] = acc


def _out_proj(x2d, ya, yb, yc, w_out_bf16):
    n, d = x2d.shape
    row = lambda i: (i, 0)
    return pl.pallas_call(
        _out_proj_kernel,
        out_shape=jax.ShapeDtypeStruct((n, d), jnp.float32),
        grid=(n // OUT_ROWS,),
        in_specs=[pl.BlockSpec((OUT_ROWS, d), row),
                  pl.BlockSpec((OUT_ROWS, ya.shape[1]), row),
                  pl.BlockSpec((OUT_ROWS, yb.shape[1]), row),
                  pl.BlockSpec((OUT_ROWS, yc.shape[1]), row),
                  pl.BlockSpec(w_out_bf16.shape, lambda i: (0, 0))],
        out_specs=pl.BlockSpec((OUT_ROWS, d), row),
        compiler_params=_params(("parallel",)),
        name="out_proj",
    )(x2d, ya, yb, yc, w_out_bf16)


def _rglru_tile(u, z, cw_ref, cb_ref, wg_ref, bg_ref, lam_ref, ubuf, a_s, b_s, hc):
    rows = a_s.shape[0]
    width = a_s.shape[1]
    ubuf[SUBLANES:SUBLANES + rows, :] = u
    xc = cb_ref[...] + cw_ref[CONV_WIDTH - 1:CONV_WIDTH, :] * u
    for k in range(CONV_WIDTH - 1):
        shift = CONV_WIDTH - 1 - k
        xc = xc + cw_ref[k:k + 1, :] * ubuf[SUBLANES - shift:SUBLANES - shift + rows, :]
    ubuf[0:SUBLANES, :] = u[rows - SUBLANES:rows, :]

    gates = _dot(xc.astype(jnp.bfloat16), wg_ref[...]) + bg_ref[...]
    r = _sigmoid(gates[:, 0:width])
    ig = _sigmoid(gates[:, width:2 * width])
    nlam = -lam_ref[...]
    softplus = jnp.maximum(nlam, 0.0) + jnp.log1p(jnp.exp(-jnp.abs(nlam)))
    log_a = (-LRU_C) * r * softplus
    a = jnp.exp(log_a)
    a_s[...] = a
    b_s[...] = jnp.sqrt(1.0 - a * a) * (ig * xc)

    row = lax.broadcasted_iota(jnp.int32, (SUBLANES, width), 0)

    h_prev = hc[...]
    for gi in range(rows // SUBLANES):
        r0 = gi * SUBLANES
        a8 = a_s[r0:r0 + SUBLANES, :]
        b8 = b_s[r0:r0 + SUBLANES, :]
        d = 1
        while d < SUBLANES:
            a_sh = jnp.where(row >= d, pltpu.roll(a8, d, 0), 1.0)
            b_sh = jnp.where(row >= d, pltpu.roll(b8, d, 0), 0.0)
            b8 = b8 + a8 * b_sh
            a8 = a8 * a_sh
            d *= 2
        h8 = b8 + a8 * h_prev
        b_s[r0:r0 + SUBLANES, :] = h8
        h_prev = h8[SUBLANES - 1:SUBLANES, :]
    hc[...] = h_prev
    return b_s[...] * _silu(z)


def _cmp_kernel(q_ref, kc_ref, vc_ref, kg_ref, pek_ref, pev_ref, wk_ref, wv_ref, qaug_ref, ocmp_ref):
    seq = q_ref.shape[1]
    n_blk = seq // CMP_STRIDE
    n_cmp = (seq - CMP_LEN) // CMP_STRIDE + 1
    n_slc = seq // SLC_LEN
    half = CMP_LEN // CMP_STRIDE
    assert half == 2 and n_blk == LANES
    kvw = NSA_KV_HEADS * HEAD_DIM
    heads_per_blk = LANES // HEAD_DIM

    acc = [jnp.zeros((n_blk, kvw), jnp.float32) for _ in range(4)]
    for l in range(CMP_STRIDE):
        xk = kc_ref[0, pl.ds(l, n_blk, stride=CMP_STRIDE), :]
        xv = vc_ref[0, pl.ds(l, n_blk, stride=CMP_STRIDE), :]
        l2 = CMP_STRIDE + l
        acc[0] = acc[0] + _dot((xk + pek_ref[l:l + 1, :]).astype(jnp.bfloat16), wk_ref[l])
        acc[1] = acc[1] + _dot((xk + pek_ref[l2:l2 + 1, :]).astype(jnp.bfloat16), wk_ref[l2])
        acc[2] = acc[2] + _dot((xv + pev_ref[l:l + 1, :]).astype(jnp.bfloat16), wv_ref[l])
        acc[3] = acc[3] + _dot((xv + pev_ref[l2:l2 + 1, :]).astype(jnp.bfloat16), wv_ref[l2])
    k_cmp = acc[0] + pltpu.roll(acc[1], n_blk - 1, 0)
    v_cmp = acc[2] + pltpu.roll(acc[3], n_blk - 1, 0)
    v_cmp_t = v_cmp.T.astype(jnp.bfloat16)
    k_n = [_head_rmsnorm(k_cmp[:, g * HEAD_DIM:(g + 1) * HEAD_DIM], kg_ref[...]).astype(jnp.bfloat16)
           for g in range(NSA_KV_HEADS)]
    v_t = [v_cmp_t[g * HEAD_DIM:(g + 1) * HEAD_DIM, :] for g in range(NSA_KV_HEADS)]

    n_idx = lax.broadcasted_iota(jnp.int32, (n_blk, TQ), 0)
    j_idx = lax.broadcasted_iota(jnp.int32, (n_slc, TQ), 0)
    oj = lax.broadcasted_iota(jnp.int32, (n_slc, n_blk), 0) * SLC_LEN
    on = lax.broadcasted_iota(jnp.int32, (n_slc, n_blk), 1) * CMP_STRIDE
    overlap_t = ((on <= oj + SLC_LEN - 1) & (on + CMP_LEN - 1 >= oj)
                 & (on < n_cmp * CMP_STRIDE)).astype(jnp.bfloat16)

    def q_tile(qi, carry):
        t0 = pl.multiple_of(qi * TQ, TQ)
        t_row = t0 + lax.broadcasted_iota(jnp.int32, (1, TQ), 1)
        valid_cmp = (n_idx * CMP_STRIDE + CMP_LEN - 1 <= t_row) & (n_idx < n_cmp)
        blk_t = t_row // SLC_LEN
        valid_slc = j_idx <= blk_t
        forced = (j_idx == 0) | (j_idx == blk_t) | (j_idx == blk_t - 1)
        for g in range(NSA_KV_HEADS):
            q_heads = []
            for r in range(NSA_GROUP):
                h = g * NSA_GROUP + r
                blk = h // heads_per_blk
                pair = q_ref[0, pl.ds(t0, TQ), blk * LANES:(blk + 1) * LANES].astype(jnp.float32)
                j = h % heads_per_blk
                q_heads.append(pair[:, j * HEAD_DIM:(j + 1) * HEAD_DIM].astype(jnp.bfloat16))
            ss = [jnp.where(valid_cmp, _dot_nt(k_n[g], qh), NEG_INF) for qh in q_heads]
            es = [jnp.where(valid_cmp, jnp.exp2(s - jnp.max(s, axis=0, keepdims=True)), 0.0)
                  for s in ss]
            ps = [e / jnp.maximum(jnp.sum(e, axis=0, keepdims=True), 1e-30) for e in es]
            for r in range(NSA_GROUP):
                h = g * NSA_GROUP + r
                ocmp_ref[0, qi, h * HEAD_DIM:(h + 1) * HEAD_DIM, :] = _dot(v_t[g], ps[r].astype(jnp.bfloat16))
            p_sum = ps[0] + ps[1] + ps[2] + ps[3]
            imp = _dot01_left(overlap_t, p_sum)
            score = jnp.where(valid_slc, jnp.where(forced, FORCE_SCORE, imp), NEG_INF)
            slabs = [score[w * SUBLANES:(w + 1) * SUBLANES, :] for w in range(n_slc // SUBLANES)]
            ranks = [jnp.zeros((SUBLANES, TQ), jnp.float32) for _ in slabs]
            sub = lax.broadcasted_iota(jnp.int32, (SUBLANES, TQ), 0)
            for jp in range(n_slc):
                other = score[jp:jp + 1, :]
                for w, slab in enumerate(slabs):
                    if w > jp // SUBLANES:
                        ahead = jnp.where(other >= slab, 1.0, 0.0)
                    elif w < jp // SUBLANES:
                        ahead = jnp.where(other > slab, 1.0, 0.0)
                    else:
                        ahead = jnp.where(sub > jp % SUBLANES, jnp.where(other >= slab, 1.0, 0.0),
                                          jnp.where(other > slab, 1.0, 0.0))
                    ranks[w] = ranks[w] + ahead
            rank = jnp.concatenate(ranks, axis=0)
            bias_t = jnp.where((rank < N_SELECT) & valid_slc, 0.0, NEG_INF)
            bias_t = jnp.concatenate([bias_t, jnp.zeros((LANES - n_slc, TQ), jnp.float32)], axis=0)
            bias = bias_t.T[:, 0:HEAD_DIM].astype(jnp.bfloat16)
            for r in range(NSA_GROUP):
                qaug_ref[0, g * NSA_GROUP + r, pl.ds(t0, TQ), :] = jnp.concatenate([q_heads[r], bias], axis=-1)
        return carry

    lax.fori_loop(0, seq // TQ, q_tile, 0)


def _nsa_cmp(nqn, kc, vc, k_g0, pe_k2, pe_v2, wk_bd, wv_bd):
    bsz, seq, qw = nqn.shape
    n_heads = qw // HEAD_DIM
    nq_t = seq // TQ
    assert seq // SLC_LEN <= HEAD_DIM
    c2 = lambda b: (0, 0)
    c3 = lambda b: (0, 0, 0)
    return pl.pallas_call(
        _cmp_kernel,
        out_shape=[jax.ShapeDtypeStruct((bsz, n_heads, seq, LANES), jnp.bfloat16),
                   jax.ShapeDtypeStruct((bsz, nq_t, qw, TQ), jnp.float32)],
        grid=(bsz,),
        in_specs=[pl.BlockSpec((1, seq, qw), lambda b: (b, 0, 0)),
                  pl.BlockSpec((1, seq, kc.shape[2]), lambda b: (b, 0, 0)),
                  pl.BlockSpec((1, seq, vc.shape[2]), lambda b: (b, 0, 0)),
                  pl.BlockSpec((1, HEAD_DIM), c2),
                  pl.BlockSpec(pe_k2.shape, c2),
                  pl.BlockSpec(pe_v2.shape, c2),
                  pl.BlockSpec(wk_bd.shape, c3),
                  pl.BlockSpec(wv_bd.shape, c3)],
        out_specs=[pl.BlockSpec((1, n_heads, seq, LANES), lambda b: (b, 0, 0, 0)),
                   pl.BlockSpec((1, nq_t, qw, TQ), lambda b: (b, 0, 0, 0))],
        compiler_params=_params(("parallel",)),
        name="nsa_compress",
    )(nqn, kc, vc, k_g0.reshape(1, HEAD_DIM), pe_k2, pe_v2, wk_bd, wv_bd)


ITEM_FIELDS = 5


def _item_table(n_q_tiles, mask_specs, window=None):
    def mask_id(spec):
        if spec not in mask_specs:
            mask_specs.append(spec)
        return mask_specs.index(spec)

    items = []
    for qi in range(n_q_tiles):
        t0 = qi * TA
        first_chunk = 0 if window is None else max(t0 - window + 1, 0) // KC
        chunks = list(range(first_chunk, (t0 + TA - 1) // KC + 1))
        for n, c in enumerate(chunks):
            needs_causal = c * KC + KC - 1 > t0
            needs_edge = window is not None and c * KC <= t0 + TA - 1 - window
            assert not (needs_causal and needs_edge)
            spec = ("causal", t0 - c * KC) if needs_causal else (
                ("edge", t0 - c * KC - window) if needs_edge else ("none", 0))
            items.append((qi, c, int(n == 0), int(n == len(chunks) - 1), mask_id(spec)))
    return np.asarray(items, np.int32).T


def _fill_masks(mask_s, mask_specs):
    @pl.when((pl.program_id(0) == 0) & (pl.program_id(1) == 0))
    def _():
        n = mask_s.shape[1]
        r = lax.broadcasted_iota(jnp.int32, (n, TA), 0)
        q = lax.broadcasted_iota(jnp.int32, (n, TA), 1)
        for i, (kind, off) in enumerate(mask_specs):
            keep = {"none": r >= 0, "causal": r <= q + off, "edge": r > q + off}[kind]
            mask_s[i] = jnp.where(keep, 0.0, NEG_INF)


def _flash_pipeline(tbl_ref, row0, n_items, n_keys, n_heads, qs_of, k_of, vt_of, s_buf, mask_s, on_last):
    heads = range(n_heads)

    def produce(t, slot, h, qs, bias):
        s = _dot_nt(k_of(h, tbl_ref[row0 + 1, t]), qs[h]) + bias
        s_buf[slot, h, 0:n_keys, :] = s
        return jnp.max(s, axis=0, keepdims=True)

    def item(t, slot, carry):
        states, col_max = carry
        qi = tbl_ref[row0, t]
        chunk = tbl_ref[row0 + 1, t]
        first = tbl_ref[row0 + 2, t] == 1
        last = tbl_ref[row0 + 3, t]
        t_next = jnp.minimum(t + 1, n_items - 1)
        qs_next = qs_of(tbl_ref[row0, t_next])
        bias_next = mask_s[tbl_ref[row0 + 4, t_next]]
        out, col_max_next = [], []
        for h in heads:
            col_max_next.append(produce(t_next, 1 - slot, h, qs_next, bias_next))
            m_prev = jnp.where(first, NEG_INF, states[h][0])
            m_new = jnp.maximum(m_prev, col_max[h])
            p = jnp.exp2(s_buf[slot, h, 0:n_keys, :] - m_new).astype(jnp.bfloat16)
            acc = jnp.exp2(m_prev - m_new) * jnp.where(first, 0.0, states[h][1])
            for j, v_t in enumerate(vt_of(h, chunk)):
                acc = acc + _dot(v_t, p[j * TQ:(j + 1) * TQ, :])
            out.append((m_new, acc))

        @pl.when(last == 1)
        def _():
            on_last(qi, [o[1] for o in out])

        return tuple(out), tuple(col_max_next)

    qs0 = qs_of(tbl_ref[row0, 0])
    bias0 = mask_s[tbl_ref[row0 + 4, 0]]
    col_max0 = tuple(produce(0, 0, h, qs0, bias0) for h in heads)
    init = tuple((jnp.full((1, TA), NEG_INF, jnp.float32), jnp.zeros((VROWS, TA), jnp.float32)) for _ in heads)
    carry = lax.fori_loop(0, n_items // 2, lambda i, c: item(2 * i + 1, 1, item(2 * i, 0, c)), (init, col_max0))
    if n_items % 2:
        item(n_items - 1, 0, carry)


def _normalise(acc):
    return acc[0:HEAD_DIM, :] / jnp.maximum(acc[HEAD_DIM:HEAD_DIM + 1, :], 1e-30)


def _store_gated(y_ref, zg_ref, outs, t0):
    for pair in range(len(outs) // 2):
        o2 = jnp.concatenate(outs[2 * pair:2 * pair + 2], axis=0).T
        lanes = slice(pair * LANES, (pair + 1) * LANES)
        zg = zg_ref[0, pl.ds(t0, TA), lanes].astype(jnp.float32)
        y_ref[0, pl.ds(t0, TA), lanes] = (o2 * zg).astype(y_ref.dtype)


def _nsa_kernel(n_slc_items, n_win_items, mask_specs, tbl_ref, qaug_ref, ks_ref, kw_ref, vt_ref, gt_ref, zg_ref,
                ocmp_ref, y_ref, s_buf, mask_s, oslc_s, owin_s):
    g_id = pl.program_id(1)
    g_row0 = pl.multiple_of(g_id * GATE_GROUP_STRIDE, GATE_GROUP_STRIDE)
    tiles_per_chunk = KC // TQ
    _fill_masks(mask_s, mask_specs)

    def qs_of(qi):
        t0 = pl.multiple_of(qi * TA, TA)
        return [qaug_ref[0, r, pl.ds(t0, TA), :] for r in range(NSA_GROUP)]

    def branch(row0, n_items, k_ref, v_rows, out_s):
        def done(qi, accs):
            for r in range(NSA_GROUP):
                out_s[qi, r] = _normalise(accs[r])

        _flash_pipeline(
            tbl_ref, row0, n_items, KC, NSA_GROUP, qs_of,
            lambda h, c: k_ref[0, pl.ds(pl.multiple_of(c * KC, KC), KC), :],
            lambda h, c: [vt_ref[0, c * tiles_per_chunk + j, v_rows, :] for j in range(tiles_per_chunk)],
            s_buf, mask_s, done)

    branch(0, n_slc_items, ks_ref, slice(0, VROWS), oslc_s)
    branch(ITEM_FIELDS, n_win_items, kw_ref, slice(VROWS, 2 * VROWS), owin_s)

    def combine(qi, carry):
        sub = range(TA // TQ)
        gates = jnp.concatenate([gt_ref[0, qi * (TA // TQ) + j, pl.ds(g_row0, GATE_GROUP_STRIDE), :] for j in sub],
                                axis=-1)
        outs = []
        for r in range(NSA_GROUP):
            o_cmp = jnp.concatenate([ocmp_ref[0, qi * (TA // TQ) + j, r * HEAD_DIM:(r + 1) * HEAD_DIM, :] for j in sub],
                                    axis=-1)
            outs.append(gates[3 * r:3 * r + 1, :] * o_cmp + gates[3 * r + 1:3 * r + 2, :] * oslc_s[qi, r]
                        + gates[3 * r + 2:3 * r + 3, :] * owin_s[qi, r])
        _store_gated(y_ref, zg_ref, outs, pl.multiple_of(qi * TA, TA))
        return carry

    lax.fori_loop(0, oslc_s.shape[0], combine, 0)


def _nsa_attend(qaug, nks, nkw, nvt, ngt, nzg, ocmp_t):
    bsz, n_heads, seq, _ = qaug.shape
    gw = NSA_GROUP * HEAD_DIM
    nq_t = seq // TQ
    nq_a = seq // TA
    assert seq % KC == 0 and seq % TA == 0 and TA % TQ == 0
    mask_specs = []
    slc_items = _item_table(nq_a, mask_specs)
    win_items = _item_table(nq_a, mask_specs, WINDOW)
    width = max(slc_items.shape[1], win_items.shape[1])
    table = np.zeros((2 * ITEM_FIELDS, width), np.int32)
    table[0:ITEM_FIELDS, :slc_items.shape[1]] = slc_items
    table[ITEM_FIELDS:, :win_items.shape[1]] = win_items
    grid_spec = pltpu.PrefetchScalarGridSpec(
        num_scalar_prefetch=1,
        grid=(bsz, NSA_KV_HEADS),
        in_specs=[pl.BlockSpec((1, NSA_GROUP, seq, LANES), lambda b, g, t: (b, g, 0, 0)),
                  pl.BlockSpec((1, seq, LANES), lambda b, g, t: (b, 0, g)),
                  pl.BlockSpec((1, seq, LANES), lambda b, g, t: (b, 0, g)),
                  pl.BlockSpec((1, nq_t, 2 * VROWS, TQ), lambda b, g, t: (b, 0, g, 0)),
                  pl.BlockSpec((1, nq_t, LANES, TQ), lambda b, g, t: (b, 0, 0, 0)),
                  pl.BlockSpec((1, seq, gw), lambda b, g, t: (b, 0, g)),
                  pl.BlockSpec((1, nq_t, gw, TQ), lambda b, g, t: (b, 0, g, 0))],
        out_specs=pl.BlockSpec((1, seq, gw), lambda b, g, t: (b, 0, g)),
        scratch_shapes=[pltpu.VMEM((2, NSA_GROUP, KC, TA), jnp.float32),
                        pltpu.VMEM((len(mask_specs), KC, TA), jnp.float32),
                        pltpu.VMEM((nq_a, NSA_GROUP, HEAD_DIM, TA), jnp.float32),
                        pltpu.VMEM((nq_a, NSA_GROUP, HEAD_DIM, TA), jnp.float32)])
    return pl.pallas_call(
        functools.partial(_nsa_kernel, slc_items.shape[1], win_items.shape[1], tuple(mask_specs)),
        out_shape=jax.ShapeDtypeStruct((bsz, seq, n_heads * HEAD_DIM), jnp.bfloat16),
        grid_spec=grid_spec,
        compiler_params=_params(("arbitrary", "arbitrary")),
        name="nsa_attend",
    )(jnp.asarray(table), qaug, nks, nkw, nvt, ngt, nzg, ocmp_t)


def _fox_kernel(n_items, mask_specs, tbl_ref, q_ref, k_ref, ex_ref, vt_ref, zg_ref, y_ref, s_buf, mask_s, o_s):
    n_heads = q_ref.shape[2] // HEAD_DIM
    heads_per_blk = LANES // HEAD_DIM
    tiles_per_chunk = KC // TQ
    _fill_masks(mask_s, mask_specs)
    lane = lax.broadcasted_iota(jnp.int32, (TA, LANES), 1)
    own = [(lane >= j * HEAD_DIM) & (lane < (j + 1) * HEAD_DIM) for j in range(heads_per_blk)]
    ones = [jnp.where((lane >= j * BIAS_TERMS) & (lane < (j + 1) * BIAS_TERMS), 1.0, 0.0).astype(jnp.bfloat16)
            for j in range(heads_per_blk)]

    def qs_of(qi):
        t0 = pl.multiple_of(qi * TA, TA)
        qs = []
        for blk in range(n_heads // heads_per_blk):
            pair = q_ref[0, pl.ds(t0, TA), blk * LANES:(blk + 1) * LANES]
            for j in range(heads_per_blk):
                qs.append(jnp.concatenate([jnp.where(own[j], pair, jnp.zeros_like(pair)), ones[j]], axis=-1))
        return qs

    def k_of(h, c):
        k0 = pl.multiple_of(c * KC, KC)
        lanes = slice((h // heads_per_blk) * LANES, (h // heads_per_blk + 1) * LANES)
        return jnp.concatenate([k_ref[0, pl.ds(k0, KC), lanes], ex_ref[0, pl.ds(k0, KC), lanes]], axis=-1)

    def done(qi, accs):
        for h in range(n_heads):
            o_s[qi, h] = _normalise(accs[h])

    _flash_pipeline(
        tbl_ref, 0, n_items, KC, n_heads, qs_of, k_of,
        lambda h, c: [vt_ref[0, c * tiles_per_chunk + j, h * VROWS:(h + 1) * VROWS, :] for j in range(tiles_per_chunk)],
        s_buf, mask_s, done)

    def finish(qi, carry):
        _store_gated(y_ref, zg_ref, [o_s[qi, h] for h in range(n_heads)], pl.multiple_of(qi * TA, TA))
        return carry

    lax.fori_loop(0, o_s.shape[0], finish, 0)


def _fox_attend(fqn, fkn, fex, fvt, fzg):
    bsz, seq, width = fqn.shape
    bw = FOX_HEADS_PER_STEP * HEAD_DIM
    assert seq % KC == 0 and seq % TA == 0 and width % bw == 0
    mask_specs = []
    table = _item_table(seq // TA, mask_specs)
    blk = pl.BlockSpec((1, seq, bw), lambda b, p, t: (b, 0, p))
    grid_spec = pltpu.PrefetchScalarGridSpec(
        num_scalar_prefetch=1,
        grid=(bsz, width // bw),
        in_specs=[blk, blk, blk,
                  pl.BlockSpec((1, seq // TQ, FOX_HEADS_PER_STEP * VROWS, TQ), lambda b, p, t: (b, 0, p, 0)),
                  blk],
        out_specs=blk,
        scratch_shapes=[pltpu.VMEM((2, FOX_HEADS_PER_STEP, KC, TA), jnp.float32),
                        pltpu.VMEM((len(mask_specs), KC, TA), jnp.float32),
                        pltpu.VMEM((seq // TA, FOX_HEADS_PER_STEP, HEAD_DIM, TA), jnp.float32)])
    return pl.pallas_call(
        functools.partial(_fox_kernel, table.shape[1], tuple(mask_specs)),
        out_shape=jax.ShapeDtypeStruct((bsz, seq, width), jnp.bfloat16),
        grid_spec=grid_spec,
        compiler_params=_params(("arbitrary", "arbitrary")),
        name="fox_attend",
    )(jnp.asarray(table), fqn, fkn, fex, fvt, fzg)


def _block_diag(blocks):
    n, r, c = blocks.shape
    eye = jnp.eye(n, dtype=blocks.dtype)
    return (eye[:, None, :, None] * blocks[:, :, None, :]).reshape(n * r, n * c)


def _layer_layout(d_model):
    lru_w = d_model // 2
    n_heads = d_model // 128
    aw = n_heads * HEAD_DIM
    kvw = NSA_KV_HEADS * HEAD_DIM
    splits = (lru_w, lru_w, aw, kvw, kvw, kvw, kvw, kvw, kvw, 3 * n_heads, aw, aw, aw, aw, n_heads, aw)
    offs = np.concatenate([[0], np.cumsum(splits)])
    names = ("lru_u", "lru_z", "nq", "kc", "vc", "ks", "vs", "kw", "vw", "gl", "nz", "fq", "fk", "fv", "fl", "fz")
    col = {n: np.arange(offs[i], offs[i + 1]) for i, n in enumerate(names)}
    nkv = []
    for g in range(NSA_KV_HEADS):
        for n in ("ks", "vs", "kw", "vw"):
            nkv.append(col[n][g * HEAD_DIM:(g + 1) * HEAD_DIM])
    misc = np.full((LANES,), -1, np.int64)
    per_group = 3 * NSA_GROUP
    for g in range(NSA_KV_HEADS):
        misc[g * GATE_GROUP_STRIDE:g * GATE_GROUP_STRIDE + per_group] = col["gl"][g * per_group:(g + 1) * per_group]
    misc[FORGET_LANE0:FORGET_LANE0 + n_heads] = col["fl"]
    segs = [("lru", np.concatenate([col["lru_u"], col["lru_z"]])),
            ("nq", col["nq"]),
            ("kc", col["kc"]), ("vc", col["vc"]),
            ("nkv", np.concatenate(nkv)),
            ("nz", col["nz"]),
            ("fq", col["fq"]), ("fk", col["fk"]), ("fv", col["fv"]), ("fz", col["fz"]),
            ("misc", misc)]
    return segs, n_heads


def _permute_w_in(w_in, segs):
    d = w_in.shape[0]
    idx = np.concatenate([c for _, c in segs])
    pieces, start = [], 0
    for end in range(1, len(idx) + 1):
        run_continues = end < len(idx) and ((idx[end] < 0 and idx[start] < 0)
                                            or (idx[start] >= 0 and idx[end] == idx[end - 1] + 1))
        if not run_continues:
            if idx[start] < 0:
                pieces.append(jnp.zeros((d, end - start), jnp.bfloat16))
            else:
                pieces.append(w_in[:, int(idx[start]):int(idx[start]) + end - start].astype(jnp.bfloat16))
            start = end
    return jnp.concatenate(pieces, axis=1)


def _hybrid_layer(x, norm_g, w_in, w_out, conv_w, conv_b, lru_wa, lru_ba, lru_wx, lru_bx, lru_lambda,
                  nsa_q_g, nsa_k_g, cmp_pe_k, cmp_pe_v, cmp_wk, cmp_wv, nsa_gate_b,
                  fox_q_g, fox_k_g, fox_f_b):
    bsz, seq, d_model = x.shape
    segs, n_heads = _layer_layout(d_model)
    aw = n_heads * HEAD_DIM
    heads_per_blk = LANES // HEAD_DIM
    assert seq % TQ == 0 and seq // CMP_STRIDE == LANES and n_heads * 3 <= 2 * GATE_GROUP_STRIDE
    assert NSA_KV_HEADS * 4 * HEAD_DIM == aw and BIAS_TERMS * heads_per_blk <= LANES
    seg, off = {}, 0
    for name, c in segs:
        seg[name] = (off, off + len(c))
        off += len(c)
    x2d = x.reshape(bsz * seq, d_model)

    ones_bd = _block_diag(jnp.ones((n_heads, HEAD_DIM, HEAD_DIM), jnp.bfloat16))
    one = jnp.ones((HEAD_DIM,), jnp.float32)
    nkv_gain = jnp.concatenate([nsa_k_g[1], one, nsa_k_g[2], one] * NSA_KV_HEADS)
    q_scale = ATTN_SCALE * LOG2E
    gain_rows = jnp.stack([jnp.tile(nsa_q_g, n_heads) * q_scale, nkv_gain,
                           jnp.tile(fox_q_g, n_heads) * q_scale, jnp.tile(fox_k_g, n_heads)])
    per_group = 3 * NSA_GROUP
    misc_bias = jnp.zeros((1, LANES), jnp.float32)
    for g in range(NSA_KV_HEADS):
        misc_bias = misc_bias.at[0, g * GATE_GROUP_STRIDE:g * GATE_GROUP_STRIDE + per_group].set(
            nsa_gate_b[g * per_group:(g + 1) * per_group])
    misc_bias = misc_bias.at[0, FORGET_LANE0:FORGET_LANE0 + n_heads].set(fox_f_b)
    place = np.zeros((BIAS_TERMS, LANES, aw), np.float32)
    for h in range(n_heads):
        for t in range(BIAS_TERMS):
            place[t, FORGET_LANE0 + h, (h // heads_per_blk) * LANES + (h % heads_per_blk) * BIAS_TERMS + t] = 1.0

    lru_w = d_model // 2
    w_gates = jnp.concatenate([_block_diag(lru_wa), _block_diag(lru_wx)], axis=1).astype(jnp.bfloat16)
    lru_params = (conv_w, conv_b.reshape(1, lru_w), w_gates, jnp.concatenate([lru_ba, lru_bx]).reshape(1, 2 * lru_w),
                  lru_lambda.reshape(1, lru_w))

    (y_lru, nqn, kc, vc, nks, nkw, nvt, ngt, nzg, fqn, fkn, fex, fvt, fzg) = _in_proj(
        x2d, norm_g, _permute_w_in(w_in, segs), seg, seq, ones_bd, gain_rows, misc_bias,
        jnp.asarray(place, jnp.bfloat16), lru_params)
    tok = lambda a: a.reshape(bsz, seq, a.shape[1])
    tiled = lambda a: a.reshape(bsz, seq // TQ, a.shape[1], TQ)

    tile2 = lambda a: jnp.tile(a, (1, NSA_KV_HEADS))
    bd2 = lambda w: jax.vmap(lambda m: _block_diag(jnp.stack([m] * NSA_KV_HEADS)))(w).astype(jnp.bfloat16)
    qaug, ocmp_t = _nsa_cmp(tok(nqn), tok(kc), tok(vc), nsa_k_g[0], tile2(cmp_pe_k), tile2(cmp_pe_v),
                            bd2(cmp_wk), bd2(cmp_wv))
    y_nsa = _nsa_attend(qaug, tok(nks), tok(nkw), tiled(nvt), tiled(ngt), tok(nzg), ocmp_t)

    y_fox = _fox_attend(tok(fqn), tok(fkn), tok(fex), tiled(fvt), tok(fzg))

    flat = lambda a: a.reshape(bsz * seq, a.shape[2])
    out = _out_proj(x2d, y_lru, flat(y_nsa), flat(y_fox), w_out.astype(jnp.bfloat16))
    return out.reshape(bsz, seq, d_model)


def kernel(x, norm_g, w_in, w_out, conv_w, conv_b, lru_wa, lru_ba, lru_wx, lru_bx, lru_lambda, nsa_q_g, nsa_k_g, cmp_pe_k, cmp_pe_v, cmp_wk, cmp_wv, nsa_gate_b, fox_q_g, fox_k_g, fox_f_b):
    for l in range(norm_g.shape[0]):
        x = _hybrid_layer(x, norm_g[l], w_in[l], w_out[l], conv_w[l], conv_b[l], lru_wa[l], lru_ba[l],
                          lru_wx[l], lru_bx[l], lru_lambda[l], nsa_q_g[l], nsa_k_g[l], cmp_pe_k[l],
                          cmp_pe_v[l], cmp_wk[l], cmp_wv[l], nsa_gate_b[l], fox_q_g[l], fox_k_g[l],
                          fox_f_b[l])
    return x
```

```python
import functools

import jax
import jax.numpy as jnp
import numpy as np
from jax import lax
from jax.experimental import pallas as pl
from jax.experimental.pallas import tpu as pltpu

HEAD_DIM = 64
LRU_BLOCKS = 8
CONV_WIDTH = 4
LRU_C = 8.0
NSA_KV_HEADS = 2
NSA_GROUP = 4
CMP_LEN = 32
CMP_STRIDE = 16
SLC_LEN = 64
N_SELECT = 16
WINDOW = 512
NORM_EPS = 1e-6
NEG_INF = -1e30
FORCE_SCORE = 1e9
ATTN_SCALE = HEAD_DIM ** -0.5
LOG2E = 1.4426950408889634

LANES = 128
SUBLANES = 8
TQ = 256
TA = 512
KC = 512
PROJ_ROWS = TQ
OUT_ROWS = 512
VMEM_LIMIT = 56 * 1024 * 1024

GATE_GROUP_STRIDE = 16
FORGET_LANE0 = 32
FOX_HEADS_PER_STEP = 8
BIAS_TERMS = 3
VROWS = 80

_NT = (((1,), (1,)), ((), ()))


def _dot(a, b):
    return jnp.dot(a, b, preferred_element_type=jnp.float32)


def _dot_nt(a, b):
    return lax.dot_general(a, b, _NT, preferred_element_type=jnp.float32)


def _split3(x):
    hi = x.astype(jnp.bfloat16)
    r = x - hi.astype(jnp.float32)
    mid = r.astype(jnp.bfloat16)
    lo = (r - mid.astype(jnp.float32)).astype(jnp.bfloat16)
    return hi, mid, lo


def _dot01_left(m01, x):
    hi, mid, lo = _split3(x)
    return _dot(m01, hi) + _dot(m01, mid) + _dot(m01, lo)


def _head_rmsnorm(x, gain):
    ms = jnp.sum(x * x, axis=-1, keepdims=True) * (1.0 / HEAD_DIM)
    return x * lax.rsqrt(ms + NORM_EPS) * gain


def _heads_rmsnorm(x, ones_bd, gain_row):
    ss = _dot((x * x).astype(jnp.bfloat16), ones_bd)
    return x * lax.rsqrt(ss * (1.0 / HEAD_DIM) + NORM_EPS) * gain_row


def _sigmoid(x):
    return 1.0 / (1.0 + jnp.exp(-x))


def _silu(x):
    return x * _sigmoid(x)


def _params(sem):
    return pltpu.CompilerParams(dimension_semantics=sem, vmem_limit_bytes=VMEM_LIMIT)


def _in_proj_kernel(seg, tiles_per_seq, x_ref, g_ref, w_ref, bd_ref, gain_ref, mb_ref, place_ref,
                    cw_ref, cb_ref, wg_ref, bg_ref, lam_ref,
                    ylru_ref, nqn_ref, kc_ref, vc_ref, nks_ref, nkw_ref, nvt_ref, ngt_ref, nzg_ref,
                    fqn_ref, fkn_ref, fex_ref, fvt_ref, fzg_ref, carry, ubuf, a_s, b_s, hc):
    i = pl.program_id(0)
    rows = x_ref.shape[0]
    x = x_ref[...]
    ms = jnp.mean(x * x, axis=-1, keepdims=True)
    h = (x * lax.rsqrt(ms + NORM_EPS) * g_ref[...]).astype(jnp.bfloat16)
    proj = lambda name: _dot(h, w_ref[:, seg[name][0]:seg[name][1]])
    bd = bd_ref[...]
    bf = jnp.bfloat16

    @pl.when(i % tiles_per_seq == 0)
    def _():
        carry[...] = jnp.zeros_like(carry)
        ubuf[0:SUBLANES, :] = jnp.zeros((SUBLANES, ubuf.shape[1]), jnp.float32)
        hc[...] = jnp.zeros_like(hc)

    uz = proj("lru")
    lru_w = uz.shape[1] // 2
    ylru_ref[...] = _rglru_tile(uz[:, 0:lru_w], uz[:, lru_w:], cw_ref, cb_ref, wg_ref, bg_ref, lam_ref,
                                ubuf, a_s, b_s, hc).astype(bf)
    kc_ref[...] = proj("kc")
    vc_ref[...] = proj("vc")
    nqn_ref[...] = _heads_rmsnorm(proj("nq"), bd, gain_ref[0:1, :]).astype(bf)
    nzg_ref[...] = _silu(proj("nz")).astype(bf)
    fqn_ref[...] = _heads_rmsnorm(proj("fq"), bd, gain_ref[2:3, :]).astype(bf)
    fkn_ref[...] = _heads_rmsnorm(proj("fk"), bd, gain_ref[3:4, :]).astype(bf)
    fzg_ref[...] = _silu(proj("fz")).astype(bf)
    pad_row = lax.broadcasted_iota(jnp.int32, (VROWS - HEAD_DIM, rows), 0)
    ones_pad = jnp.where(pad_row == 0, 1.0, 0.0).astype(bf)

    def store_vt(ref, slot, v_t):
        ref[0, slot * VROWS:slot * VROWS + HEAD_DIM, :] = v_t.astype(bf)
        ref[0, slot * VROWS + HEAD_DIM:(slot + 1) * VROWS, :] = ones_pad

    fv = proj("fv")
    for j in range(fv.shape[1] // LANES):
        fv_t = fv[:, j * LANES:(j + 1) * LANES].T
        store_vt(fvt_ref, 2 * j, fv_t[0:HEAD_DIM, :])
        store_vt(fvt_ref, 2 * j + 1, fv_t[HEAD_DIM:2 * HEAD_DIM, :])

    nkv = proj("nkv")
    nkv_n = _heads_rmsnorm(nkv, bd, gain_ref[1:2, :])
    lane = lax.broadcasted_iota(jnp.int32, (rows, LANES), 1)
    pos = (i % tiles_per_seq) * rows + lax.broadcasted_iota(jnp.int32, (rows, LANES), 0)
    onehot = jnp.where(lane - HEAD_DIM == pos // SLC_LEN, 1.0, 0.0)
    for g in range(NSA_KV_HEADS):
        b0 = 2 * g * LANES
        nks_ref[:, g * LANES:(g + 1) * LANES] = jnp.where(lane < HEAD_DIM, nkv_n[:, b0:b0 + LANES], onehot).astype(bf)
        nkw_ref[:, g * LANES:(g + 1) * LANES] = jnp.where(lane < HEAD_DIM, nkv_n[:, b0 + LANES:b0 + 2 * LANES],
                                                          0.0).astype(bf)
    for j in range(nkv.shape[1] // LANES):
        store_vt(nvt_ref, j, nkv[:, j * LANES:(j + 1) * LANES].T[HEAD_DIM:2 * HEAD_DIM, :])

    misc = proj("misc") + mb_ref[...]
    ngt_ref[0] = _sigmoid(misc).T
    log_f = jnp.minimum(misc, 0.0) - jnp.log1p(jnp.exp(-jnp.abs(misc)))

    tri =(lax.broadcasted_iota(jnp.int32, (rows, rows), 0)
           >= lax.broadcasted_iota(jnp.int32, (rows, rows), 1)).astype(bf)
    cum = _dot01_left(tri, log_f) + carry[...]
    carry[...] = cum[rows - 1:rows, :]
    terms = _split3(cum * (-LOG2E))
    fex = _dot(terms[0], place_ref[0])
    for t in range(1, BIAS_TERMS):
        fex = fex + _dot(terms[t], place_ref[t])
    fex_ref[...] = fex.astype(bf)


def _in_proj(x2d, norm_g, w_perm, seg, seq, ones_bd, gain_rows, misc_bias, place, lru_params):
    n, d = x2d.shape
    aw = seg["nq"][1] - seg["nq"][0]
    lru_w = (seg["lru"][1] - seg["lru"][0]) // 2
    rows = PROJ_ROWS
    assert n % rows == 0 and seq % rows == 0
    n_tiles = n // rows
    f32, bf = jnp.float32, jnp.bfloat16
    row_out = lambda w, dt: (jax.ShapeDtypeStruct((n, w), dt), pl.BlockSpec((rows, w), lambda i: (i, 0)))
    t_out = lambda w, dt: (jax.ShapeDtypeStruct((n_tiles, w, rows), dt), pl.BlockSpec((1, w, rows), lambda i: (i, 0, 0)))
    outs = [row_out(lru_w, bf),
            row_out(aw, bf),
            row_out(LANES, f32), row_out(LANES, f32),
            row_out(NSA_KV_HEADS * LANES, bf),
            row_out(NSA_KV_HEADS * LANES, bf),
            t_out(2 * NSA_KV_HEADS * VROWS, bf),
            t_out(LANES, f32),
            row_out(aw, bf),
            row_out(aw, bf), row_out(aw, bf), row_out(aw, bf),
            t_out(aw // HEAD_DIM * VROWS, bf),
            row_out(aw, bf)]
    const2 = lambda a: pl.BlockSpec(a.shape, lambda i: (0, 0))
    return pl.pallas_call(
        functools.partial(_in_proj_kernel, seg, seq // rows),
        out_shape=[o[0] for o in outs],
        grid=(n_tiles,),
        in_specs=[pl.BlockSpec((rows, d), lambda i: (i, 0)),
                  pl.BlockSpec((1, d), lambda i: (0, 0)),
                  const2(w_perm), const2(ones_bd), const2(gain_rows), const2(misc_bias),
                  pl.BlockSpec(place.shape, lambda i: (0, 0, 0))]
                 + [pl.BlockSpec(a.shape, lambda i, nd=a.ndim: (0,) * nd) for a in lru_params],
        out_specs=[o[1] for o in outs],
        scratch_shapes=[pltpu.VMEM((1, LANES), jnp.float32),
                        pltpu.VMEM((rows + SUBLANES, lru_w), jnp.float32),
                        pltpu.VMEM((rows, lru_w), jnp.float32),
                        pltpu.VMEM((rows, lru_w), jnp.float32),
                        pltpu.VMEM((1, lru_w), jnp.float32)],
        compiler_params=_params(("arbitrary",)),
        name="in_proj",
    )(x2d, norm_g.reshape(1, d), w_perm, ones_bd, gain_rows, misc_bias, place, *lru_params)


def _out_proj_kernel(x_ref, ya_ref, yb_ref, yc_ref, w_ref, o_ref):
    wa = ya_ref.shape[1]
    wb = yb_ref.shape[1]
    acc = x_ref[...]
    acc = acc + _dot(ya_ref[...], w_ref[0:wa, :])
    acc = acc + _dot(yb_ref[...], w_ref[wa:wa + wb, :])
    acc = acc + _dot(yc_ref[...], w_ref[wa + wb:, :])
    o_ref[...] = acc


def _out_proj(x2d, ya, yb, yc, w_out_bf16):
    n, d = x2d.shape
    row = lambda i: (i, 0)
    return pl.pallas_call(
        _out_proj_kernel,
        out_shape=jax.ShapeDtypeStruct((n, d), jnp.float32),
        grid=(n // OUT_ROWS,),
        in_specs=[pl.BlockSpec((OUT_ROWS, d), row),
                  pl.BlockSpec((OUT_ROWS, ya.shape[1]), row),
                  pl.BlockSpec((OUT_ROWS, yb.shape[1]), row),
                  pl.BlockSpec((OUT_ROWS, yc.shape[1]), row),
                  pl.BlockSpec(w_out_bf16.shape, lambda i: (0, 0))],
        out_specs=pl.BlockSpec((OUT_ROWS, d), row),
        compiler_params=_params(("parallel",)),
        name="out_proj",
    )(x2d, ya, yb, yc, w_out_bf16)


def _rglru_tile(u, z, cw_ref, cb_ref, wg_ref, bg_ref, lam_ref, ubuf, a_s, b_s, hc):
    rows = a_s.shape[0]
    width = a_s.shape[1]
    ubuf[SUBLANES:SUBLANES + rows, :] = u
    xc = cb_ref[...] + cw_ref[CONV_WIDTH - 1:CONV_WIDTH, :] * u
    for k in range(CONV_WIDTH - 1):
        shift = CONV_WIDTH - 1 - k
        xc = xc + cw_ref[k:k + 1, :] * ubuf[SUBLANES - shift:SUBLANES - shift + rows, :]
    ubuf[0:SUBLANES, :] = u[rows - SUBLANES:rows, :]

    xcb = xc.astype(jnp.bfloat16)
    slabs = [_dot(xcb[:, s * LANES:(s + 1) * LANES], wg_ref[s]) for s in range(width // LANES)]
    r = _sigmoid(jnp.concatenate([g[:, 0:LANES] for g in slabs], axis=-1) + bg_ref[:, 0:width])
    ig = _sigmoid(jnp.concatenate([g[:, LANES:2 * LANES] for g in slabs], axis=-1) + bg_ref[:, width:2 * width])
    nlam = -lam_ref[...]
    softplus = jnp.maximum(nlam, 0.0) + jnp.log1p(jnp.exp(-jnp.abs(nlam)))
    log_a = (-LRU_C) * r * softplus
    a = jnp.exp(log_a)
    a_s[...] = a
    b_s[...] = jnp.sqrt(1.0 - a * a) * (ig * xc)


    row = lax.broadcasted_iota(jnp.int32, (SUBLANES, width), 0)
    h_prev = hc[...]
    for gi in range(rows // SUBLANES):
        r0 = gi * SUBLANES
        a8 = a_s[r0:r0 + SUBLANES, :]
        b8 = b_s[r0:r0 + SUBLANES, :]
        d = 1
        while d < SUBLANES:
            a_sh = jnp.where(row >= d, pltpu.roll(a8, d, 0), 1.0)
            b_sh = jnp.where(row >= d, pltpu.roll(b8, d, 0), 0.0)
            b8 = b8 + a8 * b_sh
            a8 = a8 * a_sh
            d *= 2
        h8 = b8 + a8 * h_prev
        b_s[r0:r0 + SUBLANES, :] = h8
        h_prev = h8[SUBLANES - 1:SUBLANES, :]
    hc[...] = h_prev
    return b_s[...] * _silu(z)


def _cmp_kernel(q_ref, kc_ref, vc_ref, kg_ref, pek_ref, pev_ref, wk_ref, wv_ref, qaug_ref, ocmp_ref):
    seq = q_ref.shape[1]
    n_blk = seq // CMP_STRIDE
    n_cmp = (seq - CMP_LEN) // CMP_STRIDE + 1
    n_slc = seq // SLC_LEN
    half = CMP_LEN // CMP_STRIDE
    assert half == 2 and n_blk == LANES
    kvw = NSA_KV_HEADS * HEAD_DIM
    heads_per_blk = LANES // HEAD_DIM

    acc = [jnp.zeros((n_blk, kvw), jnp.float32) for _ in range(4)]
    for l in range(CMP_STRIDE):
        xk = kc_ref[0, pl.ds(l, n_blk, stride=CMP_STRIDE), :]
        xv = vc_ref[0, pl.ds(l, n_blk, stride=CMP_STRIDE), :]
        l2 = CMP_STRIDE + l
        acc[0] = acc[0] + _dot((xk + pek_ref[l:l + 1, :]).astype(jnp.bfloat16), wk_ref[l])
        acc[1] = acc[1] + _dot((xk + pek_ref[l2:l2 + 1, :]).astype(jnp.bfloat16), wk_ref[l2])
        acc[2] = acc[2] + _dot((xv + pev_ref[l:l + 1, :]).astype(jnp.bfloat16), wv_ref[l])
        acc[3] = acc[3] + _dot((xv + pev_ref[l2:l2 + 1, :]).astype(jnp.bfloat16), wv_ref[l2])
    k_cmp = acc[0] + pltpu.roll(acc[1], n_blk - 1, 0)
    v_cmp = acc[2] + pltpu.roll(acc[3], n_blk - 1, 0)
    v_cmp_t = v_cmp.T.astype(jnp.bfloat16)
    k_n = [_head_rmsnorm(k_cmp[:, g * HEAD_DIM:(g + 1) * HEAD_DIM], kg_ref[...]).astype(jnp.bfloat16)
           for g in range(NSA_KV_HEADS)]
    v_t = [v_cmp_t[g * HEAD_DIM:(g + 1) * HEAD_DIM, :] for g in range(NSA_KV_HEADS)]

    n_idx = lax.broadcasted_iota(jnp.int32, (n_blk, TQ), 0)
    j_idx = lax.broadcasted_iota(jnp.int32, (n_slc, TQ), 0)
    oj = lax.broadcasted_iota(jnp.int32, (n_slc, n_blk), 0) * SLC_LEN
    on = lax.broadcasted_iota(jnp.int32, (n_slc, n_blk), 1) * CMP_STRIDE
    overlap_t = ((on <= oj + SLC_LEN - 1) & (on + CMP_LEN - 1 >= oj)
                 & (on < n_cmp * CMP_STRIDE)).astype(jnp.bfloat16)

    def q_tile(qi, carry):
        t0 = pl.multiple_of(qi * TQ, TQ)
        t_row = t0 + lax.broadcasted_iota(jnp.int32, (1, TQ), 1)
        valid_cmp = (n_idx * CMP_STRIDE + CMP_LEN - 1 <= t_row) & (n_idx < n_cmp)
        blk_t = t_row // SLC_LEN
        valid_slc = j_idx <= blk_t
        forced = (j_idx == 0) | (j_idx == blk_t) | (j_idx == blk_t - 1)
        for g in range(NSA_KV_HEADS):
            q_heads = []
            for r in range(NSA_GROUP):
                h = g * NSA_GROUP + r
                blk = h // heads_per_blk
                pair = q_ref[0, pl.ds(t0, TQ), blk * LANES:(blk + 1) * LANES].astype(jnp.float32)
                j = h % heads_per_blk
                q_heads.append(pair[:, j * HEAD_DIM:(j + 1) * HEAD_DIM].astype(jnp.bfloat16))
            ss = [jnp.where(valid_cmp, _dot_nt(k_n[g], qh), NEG_INF) for qh in q_heads]
            es = [jnp.where(valid_cmp, jnp.exp2(s - jnp.max(s, axis=0, keepdims=True)), 0.0)
                  for s in ss]
            ps = [e / jnp.maximum(jnp.sum(e, axis=0, keepdims=True), 1e-30) for e in es]
            for r in range(NSA_GROUP):
                h = g * NSA_GROUP + r
                ocmp_ref[0, qi, h * HEAD_DIM:(h + 1) * HEAD_DIM, :] = _dot(v_t[g], ps[r].astype(jnp.bfloat16))
            p_sum = ps[0] + ps[1] + ps[2] + ps[3]
            imp = _dot01_left(overlap_t, p_sum)
            score = jnp.where(valid_slc, jnp.where(forced, FORCE_SCORE, imp), NEG_INF)
            slabs = [score[w * SUBLANES:(w + 1) * SUBLANES, :] for w in range(n_slc // SUBLANES)]
            ranks = [jnp.zeros((SUBLANES, TQ), jnp.float32) for _ in slabs]
            sub = lax.broadcasted_iota(jnp.int32, (SUBLANES, TQ), 0)
            for jp in range(n_slc):
                other = score[jp:jp + 1, :]
                for w, slab in enumerate(slabs):
                    if w > jp // SUBLANES:
                        ahead = jnp.where(other >= slab, 1.0, 0.0)
                    elif w < jp // SUBLANES:
                        ahead = jnp.where(other > slab, 1.0, 0.0)
                    else:
                        ahead = jnp.where(sub > jp % SUBLANES, jnp.where(other >= slab, 1.0, 0.0),
                                          jnp.where(other > slab, 1.0, 0.0))
                    ranks[w] = ranks[w] + ahead
            rank = jnp.concatenate(ranks, axis=0)
            bias_t = jnp.where((rank < N_SELECT) & valid_slc, 0.0, NEG_INF)
            bias_t = jnp.concatenate([bias_t, jnp.zeros((LANES - n_slc, TQ), jnp.float32)], axis=0)
            bias = bias_t.T[:, 0:HEAD_DIM].astype(jnp.bfloat16)
            for r in range(NSA_GROUP):
                qaug_ref[0, g * NSA_GROUP + r, pl.ds(t0, TQ), :] = jnp.concatenate([q_heads[r], bias], axis=-1)
        return carry

    lax.fori_loop(0, seq // TQ, q_tile, 0)


def _nsa_cmp(nqn, kc, vc, k_g0, pe_k2, pe_v2, wk_bd, wv_bd):
    bsz, seq, qw = nqn.shape
    n_heads = qw // HEAD_DIM
    nq_t = seq // TQ
    assert seq // SLC_LEN <= HEAD_DIM
    c2 = lambda b: (0, 0)
    c3 = lambda b: (0, 0, 0)
    return pl.pallas_call(
        _cmp_kernel,
        out_shape=[jax.ShapeDtypeStruct((bsz, n_heads, seq, LANES), jnp.bfloat16),
                   jax.ShapeDtypeStruct((bsz, nq_t, qw, TQ), jnp.float32)],
        grid=(bsz,),
        in_specs=[pl.BlockSpec((1, seq, qw), lambda b: (b, 0, 0)),
                  pl.BlockSpec((1, seq, kc.shape[2]), lambda b: (b, 0, 0)),
                  pl.BlockSpec((1, seq, vc.shape[2]), lambda b: (b, 0, 0)),
                  pl.BlockSpec((1, HEAD_DIM), c2),
                  pl.BlockSpec(pe_k2.shape, c2),
                  pl.BlockSpec(pe_v2.shape, c2),
                  pl.BlockSpec(wk_bd.shape, c3),
                  pl.BlockSpec(wv_bd.shape, c3)],
        out_specs=[pl.BlockSpec((1, n_heads, seq, LANES), lambda b: (b, 0, 0, 0)),
                   pl.BlockSpec((1, nq_t, qw, TQ), lambda b: (b, 0, 0, 0))],
        compiler_params=_params(("parallel",)),
        name="nsa_compress",
    )(nqn, kc, vc, k_g0.reshape(1, HEAD_DIM), pe_k2, pe_v2, wk_bd, wv_bd)


ITEM_FIELDS = 5


def _item_table(n_q_tiles, mask_specs, window=None):
    def mask_id(spec):
        if spec not in mask_specs:
            mask_specs.append(spec)
        return mask_specs.index(spec)

    items = []
    for qi in range(n_q_tiles):
        t0 = qi * TA
        first_chunk = 0 if window is None else max(t0 - window + 1, 0) // KC
        chunks = list(range(first_chunk, (t0 + TA - 1) // KC + 1))
        for n, c in enumerate(chunks):
            needs_causal = c * KC + KC - 1 > t0
            needs_edge = window is not None and c * KC <= t0 + TA - 1 - window
            assert not (needs_causal and needs_edge)
            spec = ("causal", t0 - c * KC) if needs_causal else (
                ("edge", t0 - c * KC - window) if needs_edge else ("none", 0))
            items.append((qi, c, int(n == 0), int(n == len(chunks) - 1), mask_id(spec)))
    return np.asarray(items, np.int32).T


def _fill_masks(mask_s, mask_specs):
    @pl.when((pl.program_id(0) == 0) & (pl.program_id(1) == 0))
    def _():
        n = mask_s.shape[1]
        r = lax.broadcasted_iota(jnp.int32, (n, TA), 0)
        q = lax.broadcasted_iota(jnp.int32, (n, TA), 1)
        for i, (kind, off) in enumerate(mask_specs):
            keep = {"none": r >= 0, "causal": r <= q + off, "edge": r > q + off}[kind]
            mask_s[i] = jnp.where(keep, 0.0, NEG_INF)


def _flash_pipeline(tbl_ref, row0, n_items, n_keys, n_heads, qs_of, k_of, vt_of, s_buf, mask_s, on_last):
    heads = range(n_heads)

    def produce(t, slot, h, qs, bias):
        s = _dot_nt(k_of(h, tbl_ref[row0 + 1, t]), qs[h]) + bias
        s_buf[slot, h, 0:n_keys, :] = s
        return jnp.max(s, axis=0, keepdims=True)

    def item(t, slot, carry):
        states, col_max = carry
        qi = tbl_ref[row0, t]
        chunk = tbl_ref[row0 + 1, t]
        first = tbl_ref[row0 + 2, t] == 1
        last = tbl_ref[row0 + 3, t]
        t_next = jnp.minimum(t + 1, n_items - 1)
        qs_next = qs_of(tbl_ref[row0, t_next])
        bias_next = mask_s[tbl_ref[row0 + 4, t_next]]
        out, col_max_next = [], []
        for h in heads:
            col_max_next.append(produce(t_next, 1 - slot, h, qs_next, bias_next))
            m_prev = jnp.where(first, NEG_INF, states[h][0])
            m_new = jnp.maximum(m_prev, col_max[h])
            p = jnp.exp2(s_buf[slot, h, 0:n_keys, :] - m_new).astype(jnp.bfloat16)
            acc = jnp.exp2(m_prev - m_new) * jnp.where(first, 0.0, states[h][1])
            for j, v_t in enumerate(vt_of(h, chunk)):
                acc = acc + _dot(v_t, p[j * TQ:(j + 1) * TQ, :])
            out.append((m_new, acc))

        @pl.when(last == 1)
        def _():
            on_last(qi, [o[1] for o in out])

        return tuple(out), tuple(col_max_next)

    qs0 = qs_of(tbl_ref[row0, 0])
    bias0 = mask_s[tbl_ref[row0 + 4, 0]]
    col_max0 = tuple(produce(0, 0, h, qs0, bias0) for h in heads)
    init = tuple((jnp.full((1, TA), NEG_INF, jnp.float32), jnp.zeros((VROWS, TA), jnp.float32)) for _ in heads)
    carry = lax.fori_loop(0, n_items // 2, lambda i, c: item(2 * i + 1, 1, item(2 * i, 0, c)), (init, col_max0))
    if n_items % 2:
        item(n_items - 1, 0, carry)


def _normalise(acc):
    return acc[0:HEAD_DIM, :] / jnp.maximum(acc[HEAD_DIM:HEAD_DIM + 1, :], 1e-30)


def _store_gated(y_ref, zg_ref, outs, t0):
    for pair in range(len(outs) // 2):
        o2 = jnp.concatenate(outs[2 * pair:2 * pair + 2], axis=0).T
        lanes = slice(pair * LANES, (pair + 1) * LANES)
        zg = zg_ref[0, pl.ds(t0, TA), lanes].astype(jnp.float32)
        y_ref[0, pl.ds(t0, TA), lanes] = (o2 * zg).astype(y_ref.dtype)


def _nsa_kernel(n_slc_items, n_win_items, mask_specs, tbl_ref, qaug_ref, ks_ref, kw_ref, vt_ref, gt_ref, zg_ref,
                ocmp_ref, y_ref, s_buf, mask_s, oslc_s, owin_s):
    g_id = pl.program_id(1)
    g_row0 = pl.multiple_of(g_id * GATE_GROUP_STRIDE, GATE_GROUP_STRIDE)
    tiles_per_chunk = KC // TQ
    _fill_masks(mask_s, mask_specs)

    def qs_of(qi):
        t0 = pl.multiple_of(qi * TA, TA)
        return [qaug_ref[0, r, pl.ds(t0, TA), :] for r in range(NSA_GROUP)]

    def branch(row0, n_items, k_ref, v_rows, out_s):
        def done(qi, accs):
            for r in range(NSA_GROUP):
                out_s[qi, r] = _normalise(accs[r])

        _flash_pipeline(
            tbl_ref, row0, n_items, KC, NSA_GROUP, qs_of,
            lambda h, c: k_ref[0, pl.ds(pl.multiple_of(c * KC, KC), KC), :],
            lambda h, c: [vt_ref[0, c * tiles_per_chunk + j, v_rows, :] for j in range(tiles_per_chunk)],
            s_buf, mask_s, done)

    branch(0, n_slc_items, ks_ref, slice(0, VROWS), oslc_s)
    branch(ITEM_FIELDS, n_win_items, kw_ref, slice(VROWS, 2 * VROWS), owin_s)

    def combine(qi, carry):
        sub = range(TA // TQ)
        gates = jnp.concatenate([gt_ref[0, qi * (TA // TQ) + j, pl.ds(g_row0, GATE_GROUP_STRIDE), :] for j in sub],
                                axis=-1)
        outs = []
        for r in range(NSA_GROUP):
            o_cmp = jnp.concatenate([ocmp_ref[0, qi * (TA // TQ) + j, r * HEAD_DIM:(r + 1) * HEAD_DIM, :] for j in sub],
                                    axis=-1)
            outs.append(gates[3 * r:3 * r + 1, :] * o_cmp + gates[3 * r + 1:3 * r + 2, :] * oslc_s[qi, r]
                        + gates[3 * r + 2:3 * r + 3, :] * owin_s[qi, r])
        _store_gated(y_ref, zg_ref, outs, pl.multiple_of(qi * TA, TA))
        return carry

    lax.fori_loop(0, oslc_s.shape[0], combine, 0)


def _nsa_attend(qaug, nks, nkw, nvt, ngt, nzg, ocmp_t):
    bsz, n_heads, seq, _ = qaug.shape
    gw = NSA_GROUP * HEAD_DIM
    nq_t = seq // TQ
    nq_a = seq // TA
    assert seq % KC == 0 and seq % TA == 0 and TA % TQ == 0
    mask_specs = []
    slc_items = _item_table(nq_a, mask_specs)
    win_items = _item_table(nq_a, mask_specs, WINDOW)
    width = max(slc_items.shape[1], win_items.shape[1])
    table = np.zeros((2 * ITEM_FIELDS, width), np.int32)
    table[0:ITEM_FIELDS, :slc_items.shape[1]] = slc_items
    table[ITEM_FIELDS:, :win_items.shape[1]] = win_items
    grid_spec = pltpu.PrefetchScalarGridSpec(
        num_scalar_prefetch=1,
        grid=(bsz, NSA_KV_HEADS),
        in_specs=[pl.BlockSpec((1, NSA_GROUP, seq, LANES), lambda b, g, t: (b, g, 0, 0)),
                  pl.BlockSpec((1, seq, LANES), lambda b, g, t: (b, 0, g)),
                  pl.BlockSpec((1, seq, LANES), lambda b, g, t: (b, 0, g)),
                  pl.BlockSpec((1, nq_t, 2 * VROWS, TQ), lambda b, g, t: (b, 0, g, 0)),
                  pl.BlockSpec((1, nq_t, LANES, TQ), lambda b, g, t: (b, 0, 0, 0)),
                  pl.BlockSpec((1, seq, gw), lambda b, g, t: (b, 0, g)),
                  pl.BlockSpec((1, nq_t, gw, TQ), lambda b, g, t: (b, 0, g, 0))],
        out_specs=pl.BlockSpec((1, seq, gw), lambda b, g, t: (b, 0, g)),
        scratch_shapes=[pltpu.VMEM((2, NSA_GROUP, KC, TA), jnp.float32),
                        pltpu.VMEM((len(mask_specs), KC, TA), jnp.float32),
                        pltpu.VMEM((nq_a, NSA_GROUP, HEAD_DIM, TA), jnp.float32),
                        pltpu.VMEM((nq_a, NSA_GROUP, HEAD_DIM, TA), jnp.float32)])
    return pl.pallas_call(
        functools.partial(_nsa_kernel, slc_items.shape[1], win_items.shape[1], tuple(mask_specs)),
        out_shape=jax.ShapeDtypeStruct((bsz, seq, n_heads * HEAD_DIM), jnp.bfloat16),
        grid_spec=grid_spec,
        compiler_params=_params(("arbitrary", "arbitrary")),
        name="nsa_attend",
    )(jnp.asarray(table), qaug, nks, nkw, nvt, ngt, nzg, ocmp_t)


def _fox_kernel(n_items, mask_specs, tbl_ref, q_ref, k_ref, ex_ref, vt_ref, zg_ref, y_ref, s_buf, mask_s, o_s):
    n_heads = q_ref.shape[2] // HEAD_DIM
    heads_per_blk = LANES // HEAD_DIM
    tiles_per_chunk = KC // TQ
    _fill_masks(mask_s, mask_specs)
    lane = lax.broadcasted_iota(jnp.int32, (TA, LANES), 1)
    own = [(lane >= j * HEAD_DIM) & (lane < (j + 1) * HEAD_DIM) for j in range(heads_per_blk)]
    ones = [jnp.where((lane >= j * BIAS_TERMS) & (lane < (j + 1) * BIAS_TERMS), 1.0, 0.0).astype(jnp.bfloat16)
            for j in range(heads_per_blk)]

    def qs_of(qi):
        t0 = pl.multiple_of(qi * TA, TA)
        qs = []
        for blk in range(n_heads // heads_per_blk):
            pair = q_ref[0, pl.ds(t0, TA), blk * LANES:(blk + 1) * LANES]
            for j in range(heads_per_blk):
                qs.append(jnp.concatenate([jnp.where(own[j], pair, jnp.zeros_like(pair)), ones[j]], axis=-1))
        return qs

    def k_of(h, c):
        k0 = pl.multiple_of(c * KC, KC)
        lanes = slice((h // heads_per_blk) * LANES, (h // heads_per_blk + 1) * LANES)
        return jnp.concatenate([k_ref[0, pl.ds(k0, KC), lanes], ex_ref[0, pl.ds(k0, KC), lanes]], axis=-1)

    def done(qi, accs):
        for h in range(n_heads):
            o_s[qi, h] = _normalise(accs[h])

    _flash_pipeline(
        tbl_ref, 0, n_items, KC, n_heads, qs_of, k_of,
        lambda h, c: [vt_ref[0, c * tiles_per_chunk + j, h * VROWS:(h + 1) * VROWS, :] for j in range(tiles_per_chunk)],
        s_buf, mask_s, done)

    def finish(qi, carry):
        _store_gated(y_ref, zg_ref, [o_s[qi, h] for h in range(n_heads)], pl.multiple_of(qi * TA, TA))
        return carry

    lax.fori_loop(0, o_s.shape[0], finish, 0)


def _fox_attend(fqn, fkn, fex, fvt, fzg):
    bsz, seq, width = fqn.shape
    bw = FOX_HEADS_PER_STEP * HEAD_DIM
    assert seq % KC == 0 and seq % TA == 0 and width % bw == 0
    mask_specs = []
    table = _item_table(seq // TA, mask_specs)
    blk = pl.BlockSpec((1, seq, bw), lambda b, p, t: (b, 0, p))
    grid_spec = pltpu.PrefetchScalarGridSpec(
        num_scalar_prefetch=1,
        grid=(bsz, width // bw),
        in_specs=[blk, blk, blk,
                  pl.BlockSpec((1, seq // TQ, FOX_HEADS_PER_STEP * VROWS, TQ), lambda b, p, t: (b, 0, p, 0)),
                  blk],
        out_specs=blk,
        scratch_shapes=[pltpu.VMEM((2, FOX_HEADS_PER_STEP, KC, TA), jnp.float32),
                        pltpu.VMEM((len(mask_specs), KC, TA), jnp.float32),
                        pltpu.VMEM((seq // TA, FOX_HEADS_PER_STEP, HEAD_DIM, TA), jnp.float32)])
    return pl.pallas_call(
        functools.partial(_fox_kernel, table.shape[1], tuple(mask_specs)),
        out_shape=jax.ShapeDtypeStruct((bsz, seq, width), jnp.bfloat16),
        grid_spec=grid_spec,
        compiler_params=_params(("arbitrary", "arbitrary")),
        name="fox_attend",
    )(jnp.asarray(table), fqn, fkn, fex, fvt, fzg)


def _block_diag(blocks):
    n, r, c = blocks.shape
    eye = jnp.eye(n, dtype=blocks.dtype)
    return (eye[:, None, :, None] * blocks[:, :, None, :]).reshape(n * r, n * c)


def _layer_layout(d_model):
    lru_w = d_model // 2
    n_heads = d_model // 128
    aw = n_heads * HEAD_DIM
    kvw = NSA_KV_HEADS * HEAD_DIM
    splits = (lru_w, lru_w, aw, kvw, kvw, kvw, kvw, kvw, kvw, 3 * n_heads, aw, aw, aw, aw, n_heads, aw)
    offs = np.concatenate([[0], np.cumsum(splits)])
    names = ("lru_u", "lru_z", "nq", "kc", "vc", "ks", "vs", "kw", "vw", "gl", "nz", "fq", "fk", "fv", "fl", "fz")
    col = {n: np.arange(offs[i], offs[i + 1]) for i, n in enumerate(names)}
    nkv = []
    for g in range(NSA_KV_HEADS):
        for n in ("ks", "vs", "kw", "vw"):
            nkv.append(col[n][g * HEAD_DIM:(g + 1) * HEAD_DIM])
    misc = np.full((LANES,), -1, np.int64)
    per_group = 3 * NSA_GROUP
    for g in range(NSA_KV_HEADS):
        misc[g * GATE_GROUP_STRIDE:g * GATE_GROUP_STRIDE + per_group] = col["gl"][g * per_group:(g + 1) * per_group]
    misc[FORGET_LANE0:FORGET_LANE0 + n_heads] = col["fl"]
    segs = [("lru", np.concatenate([col["lru_u"], col["lru_z"]])),
            ("nq", col["nq"]),
            ("kc", col["kc"]), ("vc", col["vc"]),
            ("nkv", np.concatenate(nkv)),
            ("nz", col["nz"]),
            ("fq", col["fq"]), ("fk", col["fk"]), ("fv", col["fv"]), ("fz", col["fz"]),
            ("misc", misc)]
    return segs, n_heads


def _permute_w_in(w_in, segs):
    d = w_in.shape[0]
    idx = np.concatenate([c for _, c in segs])
    pieces, start = [], 0
    for end in range(1, len(idx) + 1):
        run_continues = end < len(idx) and ((idx[end] < 0 and idx[start] < 0)
                                            or (idx[start] >= 0 and idx[end] == idx[end - 1] + 1))
        if not run_continues:
            if idx[start] < 0:
                pieces.append(jnp.zeros((d, end - start), jnp.bfloat16))
            else:
                pieces.append(w_in[:, int(idx[start]):int(idx[start]) + end - start].astype(jnp.bfloat16))
            start = end
    return jnp.concatenate(pieces, axis=1)


def _hybrid_layer(x, norm_g, w_in, w_out, conv_w, conv_b, lru_wa, lru_ba, lru_wx, lru_bx, lru_lambda,
                  nsa_q_g, nsa_k_g, cmp_pe_k, cmp_pe_v, cmp_wk, cmp_wv, nsa_gate_b,
                  fox_q_g, fox_k_g, fox_f_b):
    bsz, seq, d_model = x.shape
    segs, n_heads = _layer_layout(d_model)
    aw = n_heads * HEAD_DIM
    heads_per_blk = LANES // HEAD_DIM
    assert seq % TQ == 0 and seq // CMP_STRIDE == LANES and n_heads * 3 <= 2 * GATE_GROUP_STRIDE
    assert NSA_KV_HEADS * 4 * HEAD_DIM == aw and BIAS_TERMS * heads_per_blk <= LANES
    seg, off = {}, 0
    for name, c in segs:
        seg[name] = (off, off + len(c))
        off += len(c)
    x2d = x.reshape(bsz * seq, d_model)

    ones_bd = _block_diag(jnp.ones((n_heads, HEAD_DIM, HEAD_DIM), jnp.bfloat16))
    one = jnp.ones((HEAD_DIM,), jnp.float32)
    nkv_gain = jnp.concatenate([nsa_k_g[1], one, nsa_k_g[2], one] * NSA_KV_HEADS)
    q_scale = ATTN_SCALE * LOG2E
    gain_rows = jnp.stack([jnp.tile(nsa_q_g, n_heads) * q_scale, nkv_gain,
                           jnp.tile(fox_q_g, n_heads) * q_scale, jnp.tile(fox_k_g, n_heads)])
    per_group = 3 * NSA_GROUP
    misc_bias = jnp.zeros((1, LANES), jnp.float32)
    for g in range(NSA_KV_HEADS):
        misc_bias = misc_bias.at[0, g * GATE_GROUP_STRIDE:g * GATE_GROUP_STRIDE + per_group].set(
            nsa_gate_b[g * per_group:(g + 1) * per_group])
    misc_bias = misc_bias.at[0, FORGET_LANE0:FORGET_LANE0 + n_heads].set(fox_f_b)
    place = np.zeros((BIAS_TERMS, LANES, aw), np.float32)
    for h in range(n_heads):
        for t in range(BIAS_TERMS):
            place[t, FORGET_LANE0 + h, (h // heads_per_blk) * LANES + (h % heads_per_blk) * BIAS_TERMS + t] = 1.0

    lru_w = d_model // 2
    per_slab = LANES // (lru_w // LRU_BLOCKS)
    slab_bd = lambda w: jax.vmap(_block_diag)(w.reshape(LRU_BLOCKS // per_slab, per_slab, *w.shape[1:]))
    w_gates = jnp.concatenate([slab_bd(lru_wa), slab_bd(lru_wx)], axis=2).astype(jnp.bfloat16)
    lru_params = (conv_w, conv_b.reshape(1, lru_w), w_gates, jnp.concatenate([lru_ba, lru_bx]).reshape(1, 2 * lru_w),
                  lru_lambda.reshape(1, lru_w))

    (y_lru, nqn, kc, vc, nks, nkw, nvt, ngt, nzg, fqn, fkn, fex, fvt, fzg) = _in_proj(
        x2d, norm_g, _permute_w_in(w_in, segs), seg, seq, ones_bd, gain_rows, misc_bias,
        jnp.asarray(place, jnp.bfloat16), lru_params)
    tok = lambda a: a.reshape(bsz, seq, a.shape[1])
    tiled = lambda a: a.reshape(bsz, seq // TQ, a.shape[1], TQ)

    tile2 = lambda a: jnp.tile(a, (1, NSA_KV_HEADS))
    bd2 = lambda w: jax.vmap(lambda m: _block_diag(jnp.stack([m] * NSA_KV_HEADS)))(w).astype(jnp.bfloat16)
    qaug, ocmp_t = _nsa_cmp(tok(nqn), tok(kc), tok(vc), nsa_k_g[0], tile2(cmp_pe_k), tile2(cmp_pe_v),
                            bd2(cmp_wk), bd2(cmp_wv))
    y_nsa = _nsa_attend(qaug, tok(nks), tok(nkw), tiled(nvt), tiled(ngt), tok(nzg), ocmp_t)

    y_fox = _fox_attend(tok(fqn), tok(fkn), tok(fex), tiled(fvt), tok(fzg))

    flat = lambda a: a.reshape(bsz * seq, a.shape[2])
    out = _out_proj(x2d, y_lru, flat(y_nsa), flat(y_fox), w_out.astype(jnp.bfloat16))
    return out.reshape(bsz, seq, d_model)


def kernel(x, norm_g, w_in, w_out, conv_w, conv_b, lru_wa, lru_ba, lru_wx, lru_bx, lru_lambda, nsa_q_g, nsa_k_g, cmp_pe_k, cmp_pe_v, cmp_wk, cmp_wv, nsa_gate_b, fox_q_g, fox_k_g, fox_f_b):
    for l in range(norm_g.shape[0]):
        x = _hybrid_layer(x, norm_g[l], w_in[l], w_out[l], conv_w[l], conv_b[l], lru_wa[l], lru_ba[l],
                          lru_wx[l], lru_bx[l], lru_lambda[l], nsa_q_g[l], nsa_k_g[l], cmp_pe_k[l],
                          cmp_pe_v[l], cmp_wk[l], cmp_wv[l], nsa_gate_b[l], fox_q_g[l], fox_k_g[l],
                          fox_f_b[l])
    return x
```

```python
import functools

import jax
import jax.numpy as jnp
import numpy as np
from jax import lax
from jax.experimental import pallas as pl
from jax.experimental.pallas import tpu as pltpu

HEAD_DIM = 64
LRU_BLOCKS = 8
CONV_WIDTH = 4
LRU_C = 8.0
NSA_KV_HEADS = 2
NSA_GROUP = 4
CMP_LEN = 32
CMP_STRIDE = 16
SLC_LEN = 64
N_SELECT = 16
WINDOW = 512
NORM_EPS = 1e-6
NEG_INF = -1e30
FORCE_SCORE = 1e9
ATTN_SCALE = HEAD_DIM ** -0.5
LOG2E = 1.4426950408889634

LANES = 128
SUBLANES = 8
TQ = 256
TA = 512
KC = 512
PROJ_ROWS = TQ
OUT_ROWS = 512
VMEM_LIMIT = 56 * 1024 * 1024

GATE_GROUP_STRIDE = 16
FORGET_LANE0 = 32
FOX_HEADS_PER_STEP = 8
BIAS_TERMS = 3
VROWS = 80

_NT = (((1,), (1,)), ((), ()))


def _dot(a, b):
    return jnp.dot(a, b, preferred_element_type=jnp.float32)


def _dot_nt(a, b):
    return lax.dot_general(a, b, _NT, preferred_element_type=jnp.float32)


def _split3(x):
    hi = x.astype(jnp.bfloat16)
    r = x - hi.astype(jnp.float32)
    mid = r.astype(jnp.bfloat16)
    lo = (r - mid.astype(jnp.float32)).astype(jnp.bfloat16)
    return hi, mid, lo


def _dot01_left(m01, x):
    hi, mid, lo = _split3(x)
    return _dot(m01, hi) + _dot(m01, mid) + _dot(m01, lo)


def _head_rmsnorm(x, gain):
    ms = jnp.sum(x * x, axis=-1, keepdims=True) * (1.0 / HEAD_DIM)
    return x * lax.rsqrt(ms + NORM_EPS) * gain


def _heads_rmsnorm(x, ones_bd, gain_row):
    ss = _dot((x * x).astype(jnp.bfloat16), ones_bd)
    return x * lax.rsqrt(ss * (1.0 / HEAD_DIM) + NORM_EPS) * gain_row


def _sigmoid(x):
    return 1.0 / (1.0 + jnp.exp(-x))


def _silu(x):
    return x * _sigmoid(x)


def _params(sem):
    return pltpu.CompilerParams(dimension_semantics=sem, vmem_limit_bytes=VMEM_LIMIT)


def _in_proj_kernel(seg, tiles_per_seq, x_ref, g_ref, w_ref, bd_ref, gain_ref, mb_ref, place_ref,
                    cw_ref, cb_ref, wg_ref, bg_ref, lam_ref,
                    ylru_ref, nqn_ref, kc_ref, vc_ref, nks_ref, nkw_ref, nvt_ref, ngt_ref, nzg_ref,
                    fqn_ref, fkn_ref, fex_ref, fvt_ref, fzg_ref, carry, ubuf, a_s, b_s, hc):
    i = pl.program_id(0)
    rows = x_ref.shape[0]
    x = x_ref[...]
    ms = jnp.mean(x * x, axis=-1, keepdims=True)
    h = (x * lax.rsqrt(ms + NORM_EPS) * g_ref[...]).astype(jnp.bfloat16)
    proj = lambda name: _dot(h, w_ref[:, seg[name][0]:seg[name][1]])
    bd = bd_ref[...]
    bf = jnp.bfloat16

    @pl.when(i % tiles_per_seq == 0)
    def _():
        carry[...] = jnp.zeros_like(carry)
        ubuf[0:SUBLANES, :] = jnp.zeros((SUBLANES, ubuf.shape[1]), jnp.float32)
        hc[...] = jnp.zeros_like(hc)

    uz = proj("lru")
    lru_w = uz.shape[1] // 2
    ylru_ref[...] = _rglru_tile(uz[:, 0:lru_w], uz[:, lru_w:], cw_ref, cb_ref, wg_ref, bg_ref, lam_ref,
                                ubuf, a_s, b_s, hc).astype(bf)
    kc_ref[...] = proj("kc")
    vc_ref[...] = proj("vc")

    nqn_ref[...] = _heads_rmsnorm(proj("nq"), bd, gain_ref[0:1, :]).astype(bf)
    nzg_ref[...] = _silu(proj("nz")).astype(bf)
    fqn_ref[...] = _heads_rmsnorm(proj("fq"), bd, gain_ref[2:3, :]).astype(bf)
    fkn_ref[...] = _heads_rmsnorm(proj("fk"), bd, gain_ref[3:4, :]).astype(bf)
    fzg_ref[...] = _silu(proj("fz")).astype(bf)
    pad_row = lax.broadcasted_iota(jnp.int32, (VROWS - HEAD_DIM, rows), 0)
    ones_pad = jnp.where(pad_row == 0, 1.0, 0.0).astype(bf)

    def store_vt(ref, slot, v_t):
        ref[0, slot * VROWS:slot * VROWS + HEAD_DIM, :] = v_t.astype(bf)
        ref[0, slot * VROWS + HEAD_DIM:(slot + 1) * VROWS, :] = ones_pad

    fv = proj("fv")
    for j in range(fv.shape[1] // LANES):
        fv_t = fv[:, j * LANES:(j + 1) * LANES].T
        store_vt(fvt_ref, 2 * j, fv_t[0:HEAD_DIM, :])
        store_vt(fvt_ref, 2 * j + 1, fv_t[HEAD_DIM:2 * HEAD_DIM, :])

    nkv = proj("nkv")
    nkv_n = _heads_rmsnorm(nkv, bd, gain_ref[1:2, :])
    lane = lax.broadcasted_iota(jnp.int32, (rows, LANES), 1)
    pos = (i % tiles_per_seq) * rows + lax.broadcasted_iota(jnp.int32, (rows, LANES), 0)
    onehot = jnp.where(lane - HEAD_DIM == pos // SLC_LEN, 1.0, 0.0)
    for g in range(NSA_KV_HEADS):
        b0 = 2 * g * LANES
        nks_ref[:, g * LANES:(g + 1) * LANES] = jnp.where(lane < HEAD_DIM, nkv_n[:, b0:b0 + LANES], onehot).astype(bf)
        nkw_ref[:, g * LANES:(g + 1) * LANES] = jnp.where(lane < HEAD_DIM, nkv_n[:, b0 + LANES:b0 + 2 * LANES],
                                                          0.0).astype(bf)
    for j in range(nkv.shape[1] // LANES):
        store_vt(nvt_ref, j, nkv[:, j * LANES:(j + 1) * LANES].T[HEAD_DIM:2 * HEAD_DIM, :])

    misc = proj("misc") + mb_ref[...]
    ngt_ref[0] = _sigmoid(misc).T
    log_f = jnp.minimum(misc, 0.0) - jnp.log1p(jnp.exp(-jnp.abs(misc)))

    tri =(lax.broadcasted_iota(jnp.int32, (rows, rows), 0)
           >= lax.broadcasted_iota(jnp.int32, (rows, rows), 1)).astype(bf)
    cum = _dot01_left(tri, log_f) + carry[...]
    carry[...] = cum[rows - 1:rows, :]
    terms = _split3(cum * (-LOG2E))
    fex = _dot(terms[0], place_ref[0])
    for t in range(1, BIAS_TERMS):
        fex = fex + _dot(terms[t], place_ref[t])
    fex_ref[...] = fex.astype(bf)


def _in_proj(x2d, norm_g, w_perm, seg, seq, ones_bd, gain_rows, misc_bias, place, lru_params):
    n, d = x2d.shape
    aw = seg["nq"][1] - seg["nq"][0]
    lru_w = (seg["lru"][1] - seg["lru"][0]) // 2
    rows = PROJ_ROWS
    assert n % rows == 0 and seq % rows == 0
    n_tiles = n // rows
    f32, bf = jnp.float32, jnp.bfloat16
    row_out = lambda w, dt: (jax.ShapeDtypeStruct((n, w), dt), pl.BlockSpec((rows, w), lambda i: (i, 0)))
    t_out = lambda w, dt: (jax.ShapeDtypeStruct((n_tiles, w, rows), dt), pl.BlockSpec((1, w, rows), lambda i: (i, 0, 0)))
    outs = [row_out(lru_w, bf),
            row_out(aw, bf),
            row_out(LANES, f32), row_out(LANES, f32),
            row_out(NSA_KV_HEADS * LANES, bf),
            row_out(NSA_KV_HEADS * LANES, bf),
            t_out(2 * NSA_KV_HEADS * VROWS, bf),
            t_out(LANES, f32),
            row_out(aw, bf),
            row_out(aw, bf), row_out(aw, bf), row_out(aw, bf),
            t_out(aw // HEAD_DIM * VROWS, bf),
            row_out(aw, bf)]
    const2 = lambda a: pl.BlockSpec(a.shape, lambda i: (0, 0))
    return pl.pallas_call(
        functools.partial(_in_proj_kernel, seg, seq // rows),
        out_shape=[o[0] for o in outs],
        grid=(n_tiles,),
        in_specs=[pl.BlockSpec((rows, d), lambda i: (i, 0)),
                  pl.BlockSpec((1, d), lambda i: (0, 0)),
                  const2(w_perm), const2(ones_bd), const2(gain_rows), const2(misc_bias),
                  pl.BlockSpec(place.shape, lambda i: (0, 0, 0))]
                 + [pl.BlockSpec(a.shape, lambda i, nd=a.ndim: (0,) * nd) for a in lru_params],
        out_specs=[o[1] for o in outs],
        scratch_shapes=[pltpu.VMEM((1, LANES), jnp.float32),
                        pltpu.VMEM((rows + SUBLANES, lru_w), jnp.float32),
                        pltpu.VMEM((rows, lru_w), jnp.float32),
                        pltpu.VMEM((rows, lru_w), jnp.float32),
                        pltpu.VMEM((1, lru_w), jnp.float32)],
        compiler_params=_params(("arbitrary",)),
        name="in_proj",
    )(x2d, norm_g.reshape(1, d), w_perm, ones_bd, gain_rows, misc_bias, place, *lru_params)


def _out_proj_kernel(x_ref, ya_ref, yb_ref, yc_ref, w_ref, o_ref):
    wa = ya_ref.shape[1]
    wb = yb_ref.shape[1]
    acc = x_ref[...]
    acc = acc + _dot(ya_ref[...], w_ref[0:wa, :])
    acc = acc + _dot(yb_ref[...], w_ref[wa:wa + wb, :])
    acc = acc + _dot(yc_ref[...], w_ref[wa + wb:, :])
    o_ref[...] = acc


def _out_proj(x2d, ya, yb, yc, w_out_bf16):
    n, d = x2d.shape
    row = lambda i: (i, 0)
    return pl.pallas_call(
        _out_proj_kernel,
        out_shape=jax.ShapeDtypeStruct((n, d), jnp.float32),
        grid=(n // OUT_ROWS,),
        in_specs=[pl.BlockSpec((OUT_ROWS, d), row),
                  pl.BlockSpec((OUT_ROWS, ya.shape[1]), row),
                  pl.BlockSpec((OUT_ROWS, yb.shape[1]), row),
                  pl.BlockSpec((OUT_ROWS, yc.shape[1]), row),
                  pl.BlockSpec(w_out_bf16.shape, lambda i: (0, 0))],
        out_specs=pl.BlockSpec((OUT_ROWS, d), row),
        compiler_params=_params(("parallel",)),
        name="out_proj",
    )(x2d, ya, yb, yc, w_out_bf16)


def _rglru_tile(u, z, cw_ref, cb_ref, wg_ref, bg_ref, lam_ref, ubuf, a_s, b_s, hc):
    rows = a_s.shape[0]
    width = a_s.shape[1]
    ubuf[SUBLANES:SUBLANES + rows, :] = u
    xc = cb_ref[...] + cw_ref[CONV_WIDTH - 1:CONV_WIDTH, :] * u
    for k in range(CONV_WIDTH - 1):
        shift = CONV_WIDTH - 1 - k
        xc = xc + cw_ref[k:k + 1, :] * ubuf[SUBLANES - shift:SUBLANES - shift + rows, :]
    ubuf[0:SUBLANES, :] = u[rows - SUBLANES:rows, :]

    xcb = xc.astype(jnp.bfloat16)
    slabs = [_dot(xcb[:, s * LANES:(s + 1) * LANES], wg_ref[s]) for s in range(width // LANES)]
    r = _sigmoid(jnp.concatenate([g[:, 0:LANES] for g in slabs], axis=-1) + bg_ref[:, 0:width])
    ig = _sigmoid(jnp.concatenate([g[:, LANES:2 * LANES] for g in slabs], axis=-1) + bg_ref[:, width:2 * width])
    nlam = -lam_ref[...]
    softplus = jnp.maximum(nlam, 0.0) + jnp.log1p(jnp.exp(-jnp.abs(nlam)))
    log_a = (-LRU_C) * r * softplus
    a = jnp.exp(log_a)
    a_s[...] = a
    b_s[...] = jnp.sqrt(1.0 - a * a) * (ig * xc)


    row = lax.broadcasted_iota(jnp.int32, (SUBLANES, width), 0)
    h_prev = hc[...]
    for gi in range(rows // SUBLANES):
        r0 = gi * SUBLANES
        a8 = a_s[r0:r0 + SUBLANES, :]
        b8 = b_s[r0:r0 + SUBLANES, :]
        d = 1
        while d < SUBLANES:
            a_sh = jnp.where(row >= d, pltpu.roll(a8, d, 0), 1.0)
            b_sh = jnp.where(row >= d, pltpu.roll(b8, d, 0), 0.0)
            b8 = b8 + a8 * b_sh
            a8 = a8 * a_sh
            d *= 2
        h8 = b8 + a8 * h_prev
        b_s[r0:r0 + SUBLANES, :] = h8
        h_prev = h8[SUBLANES - 1:SUBLANES, :]
    hc[...] = h_prev
    return b_s[...] * _silu(z)


def _cmp_kernel(q_ref, kc_ref, vc_ref, kg_ref, pek_ref, pev_ref, wk_ref, wv_ref, qaug_ref, ocmp_ref):
    seq = q_ref.shape[1]
    n_blk = seq // CMP_STRIDE
    n_cmp = (seq - CMP_LEN) // CMP_STRIDE + 1
    n_slc = seq // SLC_LEN
    half = CMP_LEN // CMP_STRIDE
    assert half == 2 and n_blk == LANES
    kvw = NSA_KV_HEADS * HEAD_DIM
    heads_per_blk = LANES // HEAD_DIM

    acc = [jnp.zeros((n_blk, kvw), jnp.float32) for _ in range(4)]
    for l in range(CMP_STRIDE):
        xk = kc_ref[0, pl.ds(l, n_blk, stride=CMP_STRIDE), :]
        xv = vc_ref[0, pl.ds(l, n_blk, stride=CMP_STRIDE), :]
        l2 = CMP_STRIDE + l
        acc[0] = acc[0] + _dot((xk + pek_ref[l:l + 1, :]).astype(jnp.bfloat16), wk_ref[l])
        acc[1] = acc[1] + _dot((xk + pek_ref[l2:l2 + 1, :]).astype(jnp.bfloat16), wk_ref[l2])
        acc[2] = acc[2] + _dot((xv + pev_ref[l:l + 1, :]).astype(jnp.bfloat16), wv_ref[l])
        acc[3] = acc[3] + _dot((xv + pev_ref[l2:l2 + 1, :]).astype(jnp.bfloat16), wv_ref[l2])
    k_cmp = acc[0] + pltpu.roll(acc[1], n_blk - 1, 0)
    v_cmp = acc[2] + pltpu.roll(acc[3], n_blk - 1, 0)
    v_cmp_t = v_cmp.T.astype(jnp.bfloat16)
    k_n = [_head_rmsnorm(k_cmp[:, g * HEAD_DIM:(g + 1) * HEAD_DIM], kg_ref[...]).astype(jnp.bfloat16)
           for g in range(NSA_KV_HEADS)]
    v_t = [v_cmp_t[g * HEAD_DIM:(g + 1) * HEAD_DIM, :] for g in range(NSA_KV_HEADS)]

    n_idx = lax.broadcasted_iota(jnp.int32, (n_blk, TQ), 0)
    j_idx = lax.broadcasted_iota(jnp.int32, (n_slc, TQ), 0)
    oj = lax.broadcasted_iota(jnp.int32, (n_slc, n_blk), 0) * SLC_LEN
    on = lax.broadcasted_iota(jnp.int32, (n_slc, n_blk), 1) * CMP_STRIDE
    overlap_t = ((on <= oj + SLC_LEN - 1) & (on + CMP_LEN - 1 >= oj)
                 & (on < n_cmp * CMP_STRIDE)).astype(jnp.bfloat16)

    def q_tile(qi, carry):
        t0 = pl.multiple_of(qi * TQ, TQ)
        t_row = t0 + lax.broadcasted_iota(jnp.int32, (1, TQ), 1)
        valid_cmp = (n_idx * CMP_STRIDE + CMP_LEN - 1 <= t_row) & (n_idx < n_cmp)
        blk_t = t_row // SLC_LEN
        valid_slc = j_idx <= blk_t
        forced = (j_idx == 0) | (j_idx == blk_t) | (j_idx == blk_t - 1)
        for g in range(NSA_KV_HEADS):
            q_heads = []
            for r in range(NSA_GROUP):
                h = g * NSA_GROUP + r
                blk = h // heads_per_blk
                pair = q_ref[0, pl.ds(t0, TQ), blk * LANES:(blk + 1) * LANES].astype(jnp.float32)
                j = h % heads_per_blk
                q_heads.append(pair[:, j * HEAD_DIM:(j + 1) * HEAD_DIM].astype(jnp.bfloat16))
            ss = [jnp.where(valid_cmp, _dot_nt(k_n[g], qh), NEG_INF) for qh in q_heads]
            es = [jnp.where(valid_cmp, jnp.exp2(s - jnp.max(s, axis=0, keepdims=True)), 0.0)
                  for s in ss]
            ps = [e / jnp.maximum(jnp.sum(e, axis=0, keepdims=True), 1e-30) for e in es]
            for r in range(NSA_GROUP):
                h = g * NSA_GROUP + r
                ocmp_ref[0, qi, h * HEAD_DIM:(h + 1) * HEAD_DIM, :] = _dot(
                    v_t[g], ps[r].astype(jnp.bfloat16)).astype(ocmp_ref.dtype)
            p_sum = ps[0] + ps[1] + ps[2] + ps[3]
            imp = _dot01_left(overlap_t, p_sum)
            score = jnp.where(valid_slc, jnp.where(forced, FORCE_SCORE, imp), NEG_INF)
            slabs = [score[w * SUBLANES:(w + 1) * SUBLANES, :] for w in range(n_slc // SUBLANES)]
            ranks = [jnp.zeros((SUBLANES, TQ), jnp.float32) for _ in slabs]
            sub = lax.broadcasted_iota(jnp.int32, (SUBLANES, TQ), 0)
            for jp in range(n_slc):
                other = score[jp:jp + 1, :]
                for w, slab in enumerate(slabs):
                    if w > jp // SUBLANES:
                        ahead = jnp.where(other >= slab, 1.0, 0.0)
                    elif w < jp // SUBLANES:
                        ahead = jnp.where(other > slab, 1.0, 0.0)
                    else:
                        ahead = jnp.where(sub > jp % SUBLANES, jnp.where(other >= slab, 1.0, 0.0),
                                          jnp.where(other > slab, 1.0, 0.0))
                    ranks[w] = ranks[w] + ahead
            rank = jnp.concatenate(ranks, axis=0)
            bias_t = jnp.where((rank < N_SELECT) & valid_slc, 0.0, NEG_INF)
            bias_t = jnp.concatenate([bias_t, jnp.zeros((LANES - n_slc, TQ), jnp.float32)], axis=0)
            bias = bias_t.T[:, 0:HEAD_DIM].astype(jnp.bfloat16)
            for r in range(NSA_GROUP):
                qaug_ref[0, g * NSA_GROUP + r, pl.ds(t0, TQ), :] = jnp.concatenate([q_heads[r], bias], axis=-1)
        return carry

    lax.fori_loop(0, seq // TQ, q_tile, 0)


def _nsa_cmp(nqn, kc, vc, k_g0, pe_k2, pe_v2, wk_bd, wv_bd):
    bsz, seq, qw = nqn.shape
    n_heads = qw // HEAD_DIM
    nq_t = seq // TQ
    assert seq // SLC_LEN <= HEAD_DIM
    c2 = lambda b: (0, 0)
    c3 = lambda b: (0, 0, 0)
    return pl.pallas_call(
        _cmp_kernel,
        out_shape=[jax.ShapeDtypeStruct((bsz, n_heads, seq, LANES), jnp.bfloat16),
                   jax.ShapeDtypeStruct((bsz, nq_t, qw, TQ), jnp.bfloat16)],
        grid=(bsz,),
        in_specs=[pl.BlockSpec((1, seq, qw), lambda b: (b, 0, 0)),
                  pl.BlockSpec((1, seq, kc.shape[2]), lambda b: (b, 0, 0)),
                  pl.BlockSpec((1, seq, vc.shape[2]), lambda b: (b, 0, 0)),
                  pl.BlockSpec((1, HEAD_DIM), c2),
                  pl.BlockSpec(pe_k2.shape, c2),
                  pl.BlockSpec(pe_v2.shape, c2),
                  pl.BlockSpec(wk_bd.shape, c3),
                  pl.BlockSpec(wv_bd.shape, c3)],
        out_specs=[pl.BlockSpec((1, n_heads, seq, LANES), lambda b: (b, 0, 0, 0)),
                   pl.BlockSpec((1, nq_t, qw, TQ), lambda b: (b, 0, 0, 0))],
        compiler_params=_params(("parallel",)),
        name="nsa_compress",
    )(nqn, kc, vc, k_g0.reshape(1, HEAD_DIM), pe_k2, pe_v2, wk_bd, wv_bd)


ITEM_FIELDS = 5


def _item_table(n_q_tiles, mask_specs, window=None):
    def mask_id(spec):
        if spec not in mask_specs:
            mask_specs.append(spec)
        return mask_specs.index(spec)

    items = []
    for qi in range(n_q_tiles):
        t0 = qi * TA
        first_chunk = 0 if window is None else max(t0 - window + 1, 0) // KC
        chunks = list(range(first_chunk, (t0 + TA - 1) // KC + 1))
        for n, c in enumerate(chunks):
            needs_causal = c * KC + KC - 1 > t0
            needs_edge = window is not None and c * KC <= t0 + TA - 1 - window
            assert not (needs_causal and needs_edge)
            spec = ("causal", t0 - c * KC) if needs_causal else (
                ("edge", t0 - c * KC - window) if needs_edge else ("none", 0))
            items.append((qi, c, int(n == 0), int(n == len(chunks) - 1), mask_id(spec)))
    return np.asarray(items, np.int32).T


def _fill_masks(mask_s, mask_specs):
    @pl.when((pl.program_id(0) == 0) & (pl.program_id(1) == 0))
    def _():
        n = mask_s.shape[1]
        r = lax.broadcasted_iota(jnp.int32, (n, TA), 0)
        q = lax.broadcasted_iota(jnp.int32, (n, TA), 1)
        for i, (kind, off) in enumerate(mask_specs):
            keep = {"none": r >= 0, "causal": r <= q + off, "edge": r > q + off}[kind]
            mask_s[i] = jnp.where(keep, 0.0, NEG_INF)


def _flash_pipeline(tbl_ref, row0, n_items, n_keys, n_heads, qs_of, k_of, vt_of, s_buf, mask_s, on_last):
    heads = range(n_heads)

    def produce(t, slot, h, qs, bias):
        s = _dot_nt(k_of(h, tbl_ref[row0 + 1, t]), qs[h]) + bias
        s_buf[slot, h, 0:n_keys, :] = s
        return jnp.max(s, axis=0, keepdims=True)

    def item(t, slot, carry):
        states, col_max = carry
        qi = tbl_ref[row0, t]
        chunk = tbl_ref[row0 + 1, t]
        first = tbl_ref[row0 + 2, t] == 1
        last = tbl_ref[row0 + 3, t]
        t_next = jnp.minimum(t + 1, n_items - 1)
        qs_next = qs_of(tbl_ref[row0, t_next])
        bias_next = mask_s[tbl_ref[row0 + 4, t_next]]
        out, col_max_next = [], []
        for h in heads:
            col_max_next.append(produce(t_next, 1 - slot, h, qs_next, bias_next))
            m_prev = jnp.where(first, NEG_INF, states[h][0])
            m_new = jnp.maximum(m_prev, col_max[h])
            p = jnp.exp2(s_buf[slot, h, 0:n_keys, :] - m_new).astype(jnp.bfloat16)
            acc = jnp.exp2(m_prev - m_new) * jnp.where(first, 0.0, states[h][1])
            for j, v_t in enumerate(vt_of(h, chunk)):
                acc = acc + _dot(v_t, p[j * TQ:(j + 1) * TQ, :])
            out.append((m_new, acc))

        @pl.when(last == 1)
        def _():
            on_last(qi, [o[1] for o in out])

        return tuple(out), tuple(col_max_next)

    qs0 = qs_of(tbl_ref[row0, 0])
    bias0 = mask_s[tbl_ref[row0 + 4, 0]]
    col_max0 = tuple(produce(0, 0, h, qs0, bias0) for h in heads)
    init = tuple((jnp.full((1, TA), NEG_INF, jnp.float32), jnp.zeros((VROWS, TA), jnp.float32)) for _ in heads)
    carry = lax.fori_loop(0, n_items // 2, lambda i, c: item(2 * i + 1, 1, item(2 * i, 0, c)), (init, col_max0))
    if n_items % 2:
        item(n_items - 1, 0, carry)


def _normalise(acc):
    return acc[0:HEAD_DIM, :] / jnp.maximum(acc[HEAD_DIM:HEAD_DIM + 1, :], 1e-30)


def _store_gated(y_ref, zg_ref, outs, t0):
    for pair in range(len(outs) // 2):
        o2 = jnp.concatenate(outs[2 * pair:2 * pair + 2], axis=0).T
        lanes = slice(pair * LANES, (pair + 1) * LANES)
        zg = zg_ref[0, pl.ds(t0, TA), lanes].astype(jnp.float32)
        y_ref[0, pl.ds(t0, TA), lanes] = (o2 * zg).astype(y_ref.dtype)


def _nsa_kernel(n_slc_items, n_win_items, mask_specs, tbl_ref, qaug_ref, ks_ref, kw_ref, vt_ref, gt_ref, zg_ref,
                ocmp_ref, y_ref, s_buf, mask_s, oslc_s, owin_s):
    n_heads = qaug_ref.shape[1]
    tiles_per_chunk = KC // TQ
    _fill_masks(mask_s, mask_specs)

    def qs_of(qi):
        t0 = pl.multiple_of(qi * TA, TA)
        return [qaug_ref[0, h, pl.ds(t0, TA), :] for h in range(n_heads)]

    def branch(row0, n_items, k_ref, branch_id, out_s):
        def done(qi, accs):
            for h in range(n_heads):
                out_s[qi, h] = _normalise(accs[h]).astype(out_s.dtype)

        def k_of(h, c):
            g = h // NSA_GROUP
            return k_ref[0, pl.ds(pl.multiple_of(c * KC, KC), KC), g * LANES:(g + 1) * LANES]

        def vt_of(h, c):
            r0 = (2 * (h // NSA_GROUP) + branch_id) * VROWS
            return [vt_ref[0, c * tiles_per_chunk + j, r0:r0 + VROWS, :] for j in range(tiles_per_chunk)]

        _flash_pipeline(tbl_ref, row0, n_items, KC, n_heads, qs_of, k_of, vt_of, s_buf, mask_s, done)

    branch(0, n_slc_items, ks_ref, 0, oslc_s)
    branch(ITEM_FIELDS, n_win_items, kw_ref, 1, owin_s)

    def combine(qi, carry):
        sub = range(TA // TQ)
        gates = jnp.concatenate([gt_ref[0, qi * (TA // TQ) + j, 0:NSA_KV_HEADS * GATE_GROUP_STRIDE, :] for j in sub],
                                axis=-1)
        outs = []
        for h in range(n_heads):
            g0 = (h // NSA_GROUP) * GATE_GROUP_STRIDE + 3 * (h % NSA_GROUP)
            o_cmp = jnp.concatenate([ocmp_ref[0, qi * (TA // TQ) + j, h * HEAD_DIM:(h + 1) * HEAD_DIM, :] for j in sub],
                                    axis=-1).astype(jnp.float32)
            outs.append(gates[g0:g0 + 1, :] * o_cmp + gates[g0 + 1:g0 + 2, :] * oslc_s[qi, h].astype(jnp.float32)
                        + gates[g0 + 2:g0 + 3, :] * owin_s[qi, h].astype(jnp.float32))
        _store_gated(y_ref, zg_ref, outs, pl.multiple_of(qi * TA, TA))
        return carry

    lax.fori_loop(0, oslc_s.shape[0], combine, 0)


def _nsa_attend(qaug, nks, nkw, nvt, ngt, nzg, ocmp_t):
    bsz, n_heads, seq, _ = qaug.shape
    gw = NSA_GROUP * HEAD_DIM
    nq_t = seq // TQ
    nq_a = seq // TA
    assert seq % KC == 0 and seq % TA == 0 and TA % TQ == 0
    mask_specs = []
    slc_items = _item_table(nq_a, mask_specs)
    win_items = _item_table(nq_a, mask_specs, WINDOW)
    width = max(slc_items.shape[1], win_items.shape[1])
    table = np.zeros((2 * ITEM_FIELDS, width), np.int32)
    table[0:ITEM_FIELDS, :slc_items.shape[1]] = slc_items
    table[ITEM_FIELDS:, :win_items.shape[1]] = win_items
    grid_spec = pltpu.PrefetchScalarGridSpec(
        num_scalar_prefetch=1,
        grid=(bsz, 1),
        in_specs=[pl.BlockSpec((1, n_heads, seq, LANES), lambda b, g, t: (b, 0, 0, 0)),
                  pl.BlockSpec((1, seq, NSA_KV_HEADS * LANES), lambda b, g, t: (b, 0, 0)),
                  pl.BlockSpec((1, seq, NSA_KV_HEADS * LANES), lambda b, g, t: (b, 0, 0)),
                  pl.BlockSpec((1, nq_t, 2 * NSA_KV_HEADS * VROWS, TQ), lambda b, g, t: (b, 0, 0, 0)),
                  pl.BlockSpec((1, nq_t, LANES, TQ), lambda b, g, t: (b, 0, 0, 0)),
                  pl.BlockSpec((1, seq, NSA_KV_HEADS * gw), lambda b, g, t: (b, 0, 0)),
                  pl.BlockSpec((1, nq_t, NSA_KV_HEADS * gw, TQ), lambda b, g, t: (b, 0, 0, 0))],
        out_specs=pl.BlockSpec((1, seq, NSA_KV_HEADS * gw), lambda b, g, t: (b, 0, 0)),
        scratch_shapes=[pltpu.VMEM((2, n_heads, KC, TA), jnp.float32),
                        pltpu.VMEM((len(mask_specs), KC, TA), jnp.float32),
                        pltpu.VMEM((nq_a, n_heads, HEAD_DIM, TA), jnp.bfloat16),
                        pltpu.VMEM((nq_a, n_heads, HEAD_DIM, TA), jnp.bfloat16)])
    return pl.pallas_call(
        functools.partial(_nsa_kernel, slc_items.shape[1], win_items.shape[1], tuple(mask_specs)),
        out_shape=jax.ShapeDtypeStruct((bsz, seq, n_heads * HEAD_DIM), jnp.bfloat16),
        grid_spec=grid_spec,
        compiler_params=_params(("arbitrary", "arbitrary")),
        name="nsa_attend",
    )(jnp.asarray(table), qaug, nks, nkw, nvt, ngt, nzg, ocmp_t)


def _fox_kernel(n_items, mask_specs, tbl_ref, q_ref, k_ref, ex_ref, vt_ref, zg_ref, y_ref, s_buf, mask_s, o_s):
    n_heads = q_ref.shape[2] // HEAD_DIM
    heads_per_blk = LANES // HEAD_DIM
    tiles_per_chunk = KC // TQ
    _fill_masks(mask_s, mask_specs)
    lane = lax.broadcasted_iota(jnp.int32, (TA, LANES), 1)
    own = [(lane >= j * HEAD_DIM) & (lane < (j + 1) * HEAD_DIM) for j in range(heads_per_blk)]
    ones = [jnp.where((lane >= j * BIAS_TERMS) & (lane < (j + 1) * BIAS_TERMS), 1.0, 0.0).astype(jnp.bfloat16)
            for j in range(heads_per_blk)]

    def qs_of(qi):
        t0 = pl.multiple_of(qi * TA, TA)
        qs = []
        for blk in range(n_heads // heads_per_blk):
            pair = q_ref[0, pl.ds(t0, TA), blk * LANES:(blk + 1) * LANES]
            for j in range(heads_per_blk):
                qs.append(jnp.concatenate([jnp.where(own[j], pair, jnp.zeros_like(pair)), ones[j]], axis=-1))
        return qs

    def k_of(h, c):
        k0 = pl.multiple_of(c * KC, KC)
        lanes = slice((h // heads_per_blk) * LANES, (h // heads_per_blk + 1) * LANES)
        return jnp.concatenate([k_ref[0, pl.ds(k0, KC), lanes], ex_ref[0, pl.ds(k0, KC), lanes]], axis=-1)

    def done(qi, accs):
        for h in range(n_heads):
            o_s[qi, h] = _normalise(accs[h])

    _flash_pipeline(
        tbl_ref, 0, n_items, KC, n_heads, qs_of, k_of,
        lambda h, c: [vt_ref[0, c * tiles_per_chunk + j, h * VROWS:(h + 1) * VROWS, :] for j in range(tiles_per_chunk)],
        s_buf, mask_s, done)

    def finish(qi, carry):
        _store_gated(y_ref, zg_ref, [o_s[qi, h] for h in range(n_heads)], pl.multiple_of(qi * TA, TA))
        return carry

    lax.fori_loop(0, o_s.shape[0], finish, 0)


def _fox_attend(fqn, fkn, fex, fvt, fzg):
    bsz, seq, width = fqn.shape
    bw = FOX_HEADS_PER_STEP * HEAD_DIM
    assert seq % KC == 0 and seq % TA == 0 and width % bw == 0
    mask_specs = []
    table = _item_table(seq // TA, mask_specs)
    blk = pl.BlockSpec((1, seq, bw), lambda b, p, t: (b, 0, p))
    grid_spec = pltpu.PrefetchScalarGridSpec(
        num_scalar_prefetch=1,
        grid=(bsz, width // bw),
        in_specs=[blk, blk, blk,
                  pl.BlockSpec((1, seq // TQ, FOX_HEADS_PER_STEP * VROWS, TQ), lambda b, p, t: (b, 0, p, 0)),
                  blk],
        out_specs=blk,
        scratch_shapes=[pltpu.VMEM((2, FOX_HEADS_PER_STEP, KC, TA), jnp.float32),
                        pltpu.VMEM((len(mask_specs), KC, TA), jnp.float32),
                        pltpu.VMEM((seq // TA, FOX_HEADS_PER_STEP, HEAD_DIM, TA), jnp.float32)])
    return pl.pallas_call(
        functools.partial(_fox_kernel, table.shape[1], tuple(mask_specs)),
        out_shape=jax.ShapeDtypeStruct((bsz, seq, width), jnp.bfloat16),
        grid_spec=grid_spec,
        compiler_params=_params(("arbitrary", "arbitrary")),
        name="fox_attend",
    )(jnp.asarray(table), fqn, fkn, fex, fvt, fzg)


def _block_diag(blocks):
    n, r, c = blocks.shape
    eye = jnp.eye(n, dtype=blocks.dtype)
    return (eye[:, None, :, None] * blocks[:, :, None, :]).reshape(n * r, n * c)


def _layer_layout(d_model):
    lru_w = d_model // 2
    n_heads = d_model // 128
    aw = n_heads * HEAD_DIM
    kvw = NSA_KV_HEADS * HEAD_DIM
    splits = (lru_w, lru_w, aw, kvw, kvw, kvw, kvw, kvw, kvw, 3 * n_heads, aw, aw, aw, aw, n_heads, aw)
    offs = np.concatenate([[0], np.cumsum(splits)])
    names = ("lru_u", "lru_z", "nq", "kc", "vc", "ks", "vs", "kw", "vw", "gl", "nz", "fq", "fk", "fv", "fl", "fz")
    col = {n: np.arange(offs[i], offs[i + 1]) for i, n in enumerate(names)}
    nkv = []
    for g in range(NSA_KV_HEADS):
        for n in ("ks", "vs", "kw", "vw"):
            nkv.append(col[n][g * HEAD_DIM:(g + 1) * HEAD_DIM])
    misc = np.full((LANES,), -1, np.int64)
    per_group = 3 * NSA_GROUP
    for g in range(NSA_KV_HEADS):
        misc[g * GATE_GROUP_STRIDE:g * GATE_GROUP_STRIDE + per_group] = col["gl"][g * per_group:(g + 1) * per_group]
    misc[FORGET_LANE0:FORGET_LANE0 + n_heads] = col["fl"]
    segs = [("lru", np.concatenate([col["lru_u"], col["lru_z"]])),
            ("nq", col["nq"]),
            ("kc", col["kc"]), ("vc", col["vc"]),
            ("nkv", np.concatenate(nkv)),
            ("nz", col["nz"]),
            ("fq", col["fq"]), ("fk", col["fk"]), ("fv", col["fv"]), ("fz", col["fz"]),
            ("misc", misc)]
    return segs, n_heads


def _permute_w_in(w_in, segs):
    d = w_in.shape[0]
    idx = np.concatenate([c for _, c in segs])
    pieces, start = [], 0
    for end in range(1, len(idx) + 1):
        run_continues = end < len(idx) and ((idx[end] < 0 and idx[start] < 0)
                                            or (idx[start] >= 0 and idx[end] == idx[end - 1] + 1))
        if not run_continues:
            if idx[start] < 0:
                pieces.append(jnp.zeros((d, end - start), jnp.bfloat16))
            else:
                pieces.append(w_in[:, int(idx[start]):int(idx[start]) + end - start].astype(jnp.bfloat16))
            start = end
    return jnp.concatenate(pieces, axis=1)


def _hybrid_layer(x, norm_g, w_in, w_out, conv_w, conv_b, lru_wa, lru_ba, lru_wx, lru_bx, lru_lambda,
                  nsa_q_g, nsa_k_g, cmp_pe_k, cmp_pe_v, cmp_wk, cmp_wv, nsa_gate_b,
                  fox_q_g, fox_k_g, fox_f_b):
    bsz, seq, d_model = x.shape
    segs, n_heads = _layer_layout(d_model)
    aw = n_heads * HEAD_DIM
    heads_per_blk = LANES // HEAD_DIM
    assert seq % TQ == 0 and seq // CMP_STRIDE == LANES and n_heads * 3 <= 2 * GATE_GROUP_STRIDE
    assert NSA_KV_HEADS * 4 * HEAD_DIM == aw and BIAS_TERMS * heads_per_blk <= LANES
    seg, off = {}, 0
    for name, c in segs:
        seg[name] = (off, off + len(c))
        off += len(c)
    x2d = x.reshape(bsz * seq, d_model)

    ones_bd = _block_diag(jnp.ones((n_heads, HEAD_DIM, HEAD_DIM), jnp.bfloat16))
    one = jnp.ones((HEAD_DIM,), jnp.float32)
    nkv_gain = jnp.concatenate([nsa_k_g[1], one, nsa_k_g[2], one] * NSA_KV_HEADS)
    q_scale = ATTN_SCALE * LOG2E
    gain_rows = jnp.stack([jnp.tile(nsa_q_g, n_heads) * q_scale, nkv_gain,
                           jnp.tile(fox_q_g, n_heads) * q_scale, jnp.tile(fox_k_g, n_heads)])
    per_group = 3 * NSA_GROUP
    misc_bias = jnp.zeros((1, LANES), jnp.float32)
    for g in range(NSA_KV_HEADS):
        misc_bias = misc_bias.at[0, g * GATE_GROUP_STRIDE:g * GATE_GROUP_STRIDE + per_group].set(
            nsa_gate_b[g * per_group:(g + 1) * per_group])
    misc_bias = misc_bias.at[0, FORGET_LANE0:FORGET_LANE0 + n_heads].set(fox_f_b)
    place = np.zeros((BIAS_TERMS, LANES, aw), np.float32)
    for h in range(n_heads):
        for t in range(BIAS_TERMS):
            place[t, FORGET_LANE0 + h, (h // heads_per_blk) * LANES + (h % heads_per_blk) * BIAS_TERMS + t] = 1.0

    lru_w = d_model // 2
    per_slab = LANES // (lru_w // LRU_BLOCKS)
    slab_bd = lambda w: jax.vmap(_block_diag)(w.reshape(LRU_BLOCKS // per_slab, per_slab, *w.shape[1:]))
    w_gates = jnp.concatenate([slab_bd(lru_wa), slab_bd(lru_wx)], axis=2).astype(jnp.bfloat16)
    lru_params = (conv_w, conv_b.reshape(1, lru_w), w_gates, jnp.concatenate([lru_ba, lru_bx]).reshape(1, 2 * lru_w),
                  lru_lambda.reshape(1, lru_w))

    (y_lru, nqn, kc, vc, nks, nkw, nvt, ngt, nzg, fqn, fkn, fex, fvt, fzg) = _in_proj(
        x2d, norm_g, _permute_w_in(w_in, segs), seg, seq, ones_bd, gain_rows, misc_bias,
        jnp.asarray(place, jnp.bfloat16), lru_params)
    tok = lambda a: a.reshape(bsz, seq, a.shape[1])
    tiled = lambda a: a.reshape(bsz, seq // TQ, a.shape[1], TQ)

    tile2 = lambda a: jnp.tile(a, (1, NSA_KV_HEADS))
    bd2 = lambda w: jax.vmap(lambda m: _block_diag(jnp.stack([m] * NSA_KV_HEADS)))(w).astype(jnp.bfloat16)
    qaug, ocmp_t = _nsa_cmp(tok(nqn), tok(kc), tok(vc), nsa_k_g[0], tile2(cmp_pe_k), tile2(cmp_pe_v),
                            bd2(cmp_wk), bd2(cmp_wv))
    y_nsa = _nsa_attend(qaug, tok(nks), tok(nkw), tiled(nvt), tiled(ngt), tok(nzg), ocmp_t)

    y_fox = _fox_attend(tok(fqn), tok(fkn), tok(fex), tiled(fvt), tok(fzg))

    flat = lambda a: a.reshape(bsz * seq, a.shape[2])
    out = _out_proj(x2d, y_lru, flat(y_nsa), flat(y_fox), w_out.astype(jnp.bfloat16))
    return out.reshape(bsz, seq, d_model)


def kernel(x, norm_g, w_in, w_out, conv_w, conv_b, lru_wa, lru_ba, lru_wx, lru_bx, lru_lambda, nsa_q_g, nsa_k_g, cmp_pe_k, cmp_pe_v, cmp_wk, cmp_wv, nsa_gate_b, fox_q_g, fox_k_g, fox_f_b):
    for l in range(norm_g.shape[0]):
        x = _hybrid_layer(x, norm_g[l], w_in[l], w_out[l], conv_w[l], conv_b[l], lru_wa[l], lru_ba[l],
                          lru_wx[l], lru_bx[l], lru_lambda[l], nsa_q_g[l], nsa_k_g[l], cmp_pe_k[l],
                          cmp_pe_v[l], cmp_wk[l], cmp_wv[l], nsa_gate_b[l], fox_q_g[l], fox_k_g[l],
                          fox_f_b[l])
    return x
```
